```python
import jax, jax.numpy as jnp
from jax import lax
import numpy as np

D_MODEL = 2048
BATCH = 2
SEQ = 4096
DEPTH = 4
DEC_BATCH = 16
DEC_SEQ = 2048
PAST_LEN = 128

N_MIXERS = 2
N_POOL_LAYERS = (DEPTH + 1) // 2
N_ATTN_LAYERS = DEPTH // 2
POOL_WINDOWS = (2, 4, 8, 16)
POOL_GROUPS = 4
POOL_CH = D_MODEL // POOL_GROUPS
HEAD_DIM = 128
N_HEADS = D_MODEL // HEAD_DIM
N_KV_HEADS = 4
GQA_GROUP = N_HEADS // N_KV_HEADS
WINDOW = 128
BLOCK = 128
Q_DIM = N_HEADS * HEAD_DIM
KV_DIM = N_KV_HEADS * HEAD_DIM
QKV_DIM = Q_DIM + 2 * KV_DIM
N_KEYS = 128
N_EXPERTS = N_KEYS * N_KEYS
PEER_HEADS = 8
PEER_TOPK = 16
PEER_KEY_DIM = 256
PEER_HALF = PEER_KEY_DIM // 2
PEER_CHUNK = 128
ALPHA = (2 * DEPTH) ** 0.25
BETA = (8 * DEPTH) ** -0.25
LN_EPS = 1e-5
NEG = -1e30

kernel_name = "hybrid_pool_swa_peer_encoder"


def _alibi_slopes():
    return np.array([2.0 ** (-8.0 * (h + 1) / N_HEADS) for h in range(N_HEADS)], dtype=np.float32)


def layer_norm(x, g, b):
    xf = x.astype(jnp.float32)
    mu = jnp.mean(xf, axis=-1, keepdims=True)
    var = jnp.mean(jnp.square(xf - mu), axis=-1, keepdims=True)
    y = (xf - mu) * lax.rsqrt(var + LN_EPS) * g.astype(jnp.float32) + b.astype(jnp.float32)
    return y.astype(x.dtype)


def pool_mixer(h, w_grp, scale, w_o):
    B, S, D = h.shape
    hf = h.astype(jnp.float32)
    cs = jnp.pad(jnp.cumsum(hf, axis=1), ((0, 0), (1, 0), (0, 0)))
    t = np.arange(S)
    outs = []
    for g, w in enumerate(POOL_WINDOWS):
        lo = np.maximum(t - w // 2, 0)
        hi = np.minimum(t + w // 2, S)
        cnt = (hi - lo).astype(np.float32)[None, :, None]
        sl = slice(g * POOL_CH, (g + 1) * POOL_CH)
        csg = cs[..., sl]
        mean = (csg[:, hi] - csg[:, lo]) / cnt
        outs.append(mean - hf[..., sl])
    pooled = jnp.stack(outs, axis=2).astype(h.dtype)
    mixed = jnp.einsum('bsgc,gcd->bsgd', pooled, w_grp).reshape(B, S, D)
    return (mixed * scale) @ w_o


def window_attention(h, w_qkv, sink, w_o):
    B, S, D = h.shape
    nb = S // BLOCK
    qkv = h @ w_qkv
    q = qkv[..., :Q_DIM].reshape(B, nb, BLOCK, N_KV_HEADS, GQA_GROUP, HEAD_DIM)
    k = qkv[..., Q_DIM:Q_DIM + KV_DIM].reshape(B, S, N_KV_HEADS, HEAD_DIM)
    v = qkv[..., Q_DIM + KV_DIM:].reshape(B, S, N_KV_HEADS, HEAD_DIM)

    def band(z):
        zp = jnp.pad(z, ((0, 0), (BLOCK, BLOCK), (0, 0), (0, 0))).reshape(B, nb + 2, BLOCK, N_KV_HEADS, HEAD_DIM)
        return jnp.concatenate([zp[:, :-2], zp[:, 1:-1], zp[:, 2:]], axis=2)

    kb, vb = band(k), band(v)
    s = jnp.einsum('bnqhgd,bnkhd->bnhgqk', q, kb).astype(jnp.float32) * (HEAD_DIM ** -0.5)
    rel = np.arange(3 * BLOCK)[None, :] - BLOCK - np.arange(BLOCK)[:, None]
    kpos = np.arange(nb)[:, None] * BLOCK - BLOCK + np.arange(3 * BLOCK)[None, :]
    valid = (np.abs(rel) <= WINDOW)[None] & ((kpos >= 0) & (kpos < S))[:, None, :]
    slopes = _alibi_slopes().reshape(N_KV_HEADS, GQA_GROUP)
    bias = -slopes[:, :, None, None] * np.abs(rel).astype(np.float32)[None, None]
    s = jnp.where(valid[None, :, None, None], s + bias[None, None], NEG)
    sink_l = sink.astype(jnp.float32).reshape(N_KV_HEADS, GQA_GROUP)[None, None, :, :, None, None]
    m = jnp.maximum(jnp.max(s, axis=-1, keepdims=True), sink_l)
    p = jnp.exp(s - m)
    p = p / (jnp.sum(p, axis=-1, keepdims=True) + jnp.exp(sink_l - m))
    o = jnp.einsum('bnhgqk,bnkhd->bnqhgd', p.astype(vb.dtype), vb).reshape(B, S, D)
    return o @ w_o


def peer_ffn(h, w_q, sub_keys, u, v):
    B, S, D = h.shape
    T = B * S
    xt = h.reshape(T, D)
    q = (xt @ w_q).reshape(T, PEER_HEADS, 2, PEER_HALF).astype(jnp.float32)
    sc = jnp.einsum('thcd,hckd->thck', q, sub_keys.astype(jnp.float32))
    s1, i1 = lax.top_k(sc[:, :, 0], PEER_TOPK)
    s2, i2 = lax.top_k(sc[:, :, 1], PEER_TOPK)
    cand = (s1[..., :, None] + s2[..., None, :]).reshape(T, PEER_HEADS, PEER_TOPK * PEER_TOPK)
    cidx = (i1[..., :, None] * N_KEYS + i2[..., None, :]).reshape(T, PEER_HEADS, PEER_TOPK * PEER_TOPK)
    best, pos = lax.top_k(cand, PEER_TOPK)
    eidx = jnp.take_along_axis(cidx, pos, axis=-1).reshape(T, PEER_HEADS * PEER_TOPK)
    gate = jax.nn.softmax(best, axis=-1).reshape(T, PEER_HEADS * PEER_TOPK).astype(h.dtype)
    nc = T // PEER_CHUNK

    def chunk(args):
        xc, ec, gc = args
        a = jax.nn.gelu(jnp.einsum('cd,ced->ce', xc, u[ec]), approximate=False)
        return jnp.einsum('ce,ced->cd', gc * a, v[ec])

    out = lax.map(chunk, (xt.reshape(nc, PEER_CHUNK, D),
                          eidx.reshape(nc, PEER_CHUNK, -1),
                          gate.reshape(nc, PEER_CHUNK, -1)))
    return out.reshape(B, S, D)


def trunk(x, c, ada_w, ada_b, ln1_g, ln1_b, ln2_g, ln2_b, pool_w, pool_scale, pool_wo,
          attn_wqkv, attn_sink, attn_wo, peer_wq, peer_keys, peer_u, peer_v):
    sc_in = jax.nn.silu(c)
    for i in range(DEPTH):
        mod = sc_in @ ada_w[i] + ada_b[i]
        sh1, s1, g1, sh2, s2, g2 = jnp.split(mod[:, None, :], 6, axis=-1)
        j = i // N_MIXERS
        h = x * (1 + s1) + sh1
        if i % N_MIXERS == 0:
            f = pool_mixer(h, pool_w[j], pool_scale[j], pool_wo[j])
        else:
            f = window_attention(h, attn_wqkv[j], attn_sink[j], attn_wo[j])
        x = layer_norm(ALPHA * x + g1 * f, ln1_g[i], ln1_b[i])
        h = x * (1 + s2) + sh2
        f = peer_ffn(h, peer_wq[i], peer_keys[i], peer_u[i], peer_v[i])
        x = layer_norm(ALPHA * x + g2 * f, ln2_g[i], ln2_b[i])
    return x


def setup_inputs(seed: int = 0) -> dict:
    key = jax.random.key(seed)
    ks = jax.random.split(key, 24)
    f32 = jnp.float32
    n = lambda k, shp, s: jax.random.normal(k, shp, f32) * s
    D = D_MODEL
    sd = D ** -0.5
    v_col_scale = jnp.concatenate([jnp.ones((Q_DIM + KV_DIM,), f32), jnp.full((KV_DIM,), BETA, f32)])
    return {
        "x_prompt": n(ks[0], (BATCH, SEQ, D), 1.0),
        "x_sample": n(ks[1], (DEC_BATCH, DEC_SEQ, D), 1.0),
        "c_prompt": n(ks[2], (BATCH, D), 1.0),
        "c_sample": n(ks[3], (DEC_BATCH, D), 1.0),
        "ada_w": n(ks[4], (DEPTH, D, 6 * D), 0.5 * sd),
        "ada_b": n(ks[5], (DEPTH, 6 * D), 0.02),
        "ln1_g": 1.0 + n(ks[6], (DEPTH, D), 0.02),
        "ln1_b": n(ks[7], (DEPTH, D), 0.02),
        "ln2_g": 1.0 + n(ks[8], (DEPTH, D), 0.02),
        "ln2_b": n(ks[9], (DEPTH, D), 0.02),
        "pool_w": n(ks[10], (N_POOL_LAYERS, POOL_GROUPS, POOL_CH, POOL_CH), POOL_CH ** -0.5),
        "pool_scale": 1.0 + n(ks[11], (N_POOL_LAYERS, D), 0.02),
        "pool_wo": n(ks[12], (N_POOL_LAYERS, D, D), BETA * sd),
        "attn_wqkv": n(ks[13], (N_ATTN_LAYERS, D, QKV_DIM), sd) * v_col_scale,
        "attn_sink": n(ks[14], (N_ATTN_LAYERS, N_HEADS), 0.5),
        "attn_wo": n(ks[15], (N_ATTN_LAYERS, D, D), BETA * sd),
        "peer_wq": n(ks[16], (DEPTH, D, PEER_HEADS * PEER_KEY_DIM), sd),
        "peer_keys": n(ks[17], (DEPTH, PEER_HEADS, 2, N_KEYS, PEER_HALF), PEER_HALF ** -0.5),
        "peer_u": n(ks[18], (DEPTH, N_EXPERTS, D), sd),
        "peer_v": n(ks[19], (DEPTH, N_EXPERTS, D), BETA),
    }


def reference(x_prompt, x_sample, c_prompt, c_sample, ada_w, ada_b, ln1_g, ln1_b, ln2_g, ln2_b,
              pool_w, pool_scale, pool_wo, attn_wqkv, attn_sink, attn_wo,
              peer_wq, peer_keys, peer_u, peer_v):
    y_prompt = trunk(x_prompt, c_prompt, ada_w, ada_b, ln1_g, ln1_b, ln2_g, ln2_b, pool_w, pool_scale,
                     pool_wo, attn_wqkv, attn_sink, attn_wo, peer_wq, peer_keys, peer_u, peer_v)
    y_sample = trunk(x_sample, c_sample, ada_w, ada_b, ln1_g, ln1_b, ln2_g, ln2_b, pool_w, pool_scale,
                     pool_wo, attn_wqkv, attn_sink, attn_wo, peer_wq, peer_keys, peer_u, peer_v)
    return (y_prompt, y_sample)
```

```python
import functools

import numpy as np
import jax
import jax.numpy as jnp
from jax import lax
from jax.experimental import pallas as pl
from jax.experimental.pallas import tpu as pltpu

F32 = jnp.float32
BF16 = jnp.bfloat16

D_MODEL = 2048
POOL_WINDOWS = (2, 4, 8, 16)
POOL_CH = D_MODEL // len(POOL_WINDOWS)
POOL_HALO = 8
HEAD_DIM = 128
N_HEADS = D_MODEL // HEAD_DIM
N_KV_HEADS = 4
GQA_GROUP = N_HEADS // N_KV_HEADS
WINDOW = 128
BLOCK = 128
Q_DIM = N_HEADS * HEAD_DIM
KV_DIM = N_KV_HEADS * HEAD_DIM
N_KEYS = 128
PEER_HEADS = 8
PEER_TOPK = 16
PEER_SEL = PEER_HEADS * PEER_TOPK
LN_EPS = 1e-5
NEG = -1e30

LANES = 128
SUBLANES = 8
ROW_TILES = D_MODEL // LANES
VMEM_LIMIT = 52 * 1024 * 1024

ADA_TN = 1024
TOKEN_TILE = 256
ROUTE_TILE = 128
PEER_TB = 128
PEER_NBUF = 4


def _alibi_slopes():
    return [2.0 ** (-8.0 * (h + 1) / N_HEADS) for h in range(N_HEADS)]


def _params(sem):
    return pltpu.CompilerParams(dimension_semantics=sem, vmem_limit_bytes=VMEM_LIMIT)


def _layer_norm(y, g, b):
    mu = jnp.mean(y, axis=-1, keepdims=True)
    d = y - mu
    var = jnp.mean(d * d, axis=-1, keepdims=True)
    return d * lax.rsqrt(var + LN_EPS) * g + b


def _ada_kernel(c_ref, w_ref, b_ref, o_ref):
    c = c_ref[...]
    sc = c * jax.nn.sigmoid(c)
    o_ref[0] = jnp.dot(sc.astype(BF16), w_ref[0].astype(BF16),
                       preferred_element_type=F32) + b_ref[0]


def _ada(c_all, ada_w, ada_b):
    depth, d, n = ada_w.shape
    m = c_all.shape[0]
    return pl.pallas_call(
        _ada_kernel,
        grid=(depth, n // ADA_TN),
        in_specs=[
            pl.BlockSpec((m, d), lambda l, j: (0, 0)),
            pl.BlockSpec((1, d, ADA_TN), lambda l, j: (l, 0, j)),
            pl.BlockSpec((1, 1, ADA_TN), lambda l, j: (l, 0, j)),
        ],
        out_specs=pl.BlockSpec((1, m, ADA_TN), lambda l, j: (l, 0, j)),
        out_shape=jax.ShapeDtypeStruct((depth, m, n), F32),
        compiler_params=_params(("arbitrary", "arbitrary")),
        name="ada_mod",
    )(c_all, ada_w, ada_b.reshape(depth, 1, n))


def _pool_kernel(xp_ref, x_ref, xn_ref, s_ref, sh_ref, g_ref, wg_ref, ps_ref, wo_ref,
                 lg_ref, lb_ref, o_ref, *, seq, ts, alpha):
    i = pl.program_id(1)
    n_i = pl.num_programs(1)
    s1 = 1.0 + s_ref[0]
    sh = sh_ref[0]
    x = x_ref[0]
    h = x * s1 + sh
    hp = jnp.where(i > 0, xp_ref[0] * s1 + sh, 0.0)
    hn = jnp.where(i < n_i - 1, xn_ref[0] * s1 + sh, 0.0)
    ext = jnp.concatenate([hp, h, hn], axis=0)
    n_ext = ts + 2 * POOL_HALO
    pos = i * ts + lax.broadcasted_iota(jnp.int32, (ts, 1), 0)

    def shifted(a, k):
        return pltpu.roll(a, k % n_ext, axis=0)

    f = None
    for g, w in enumerate(POOL_WINDOWS):
        cs = slice(g * POOL_CH, (g + 1) * POOL_CH)
        e = ext[:, cs]
        acc = e + shifted(e, 1)
        half = 1
        while 2 * half < w:
            acc = shifted(acc, half) + shifted(acc, -half)
            half *= 2
        win = acc[POOL_HALO:POOL_HALO + ts]
        cnt = (jnp.minimum(pos + w // 2, seq) - jnp.maximum(pos - w // 2, 0)).astype(F32)
        pooled = win / cnt - h[:, cs]
        mixed = jnp.dot(pooled.astype(BF16), wg_ref[g], preferred_element_type=F32)
        mixed = mixed * ps_ref[:, cs]
        part = jnp.dot(mixed.astype(BF16), wo_ref[cs, :], preferred_element_type=F32)
        f = part if f is None else f + part
    y = alpha * x + g_ref[0] * f
    o_ref[0] = _layer_norm(y, lg_ref[...], lb_ref[...])


def _pool_layer(x, s1, sh1, g1, wg, ps, wo, lg, lb, alpha):
    b, seq, d = x.shape
    ts = TOKEN_TILE
    per = ts // POOL_HALO
    n_halo = seq // POOL_HALO
    vec = pl.BlockSpec((1, 1, d), lambda bi, i: (bi, 0, 0))
    row = pl.BlockSpec((1, d), lambda bi, i: (0, 0))
    return pl.pallas_call(
        functools.partial(_pool_kernel, seq=seq, ts=ts, alpha=alpha),
        grid=(b, seq // ts),
        in_specs=[
            pl.BlockSpec((1, POOL_HALO, d), lambda bi, i: (bi, jnp.maximum(i * per - 1, 0), 0)),
            pl.BlockSpec((1, ts, d), lambda bi, i: (bi, i, 0)),
            pl.BlockSpec((1, POOL_HALO, d),
                         lambda bi, i: (bi, jnp.minimum((i + 1) * per, n_halo - 1), 0)),
            vec, vec, vec,
            pl.BlockSpec(wg.shape, lambda bi, i: (0, 0, 0)),
            row,
            pl.BlockSpec((d, d), lambda bi, i: (0, 0)),
            row, row,
        ],
        out_specs=pl.BlockSpec((1, ts, d), lambda bi, i: (bi, i, 0)),
        out_shape=jax.ShapeDtypeStruct((b, seq, d), F32),
        compiler_params=_params(("arbitrary", "arbitrary")),
        name="pool_layer",
    )(x, x, x, s1, sh1, g1, wg, ps, wo, lg, lb)


def _qkv_kernel(x_ref, s_ref, sh_ref, w_ref, q_ref, k_ref, v_ref):
    h = x_ref[0] * (1.0 + s_ref[0]) + sh_ref[0]
    qkv = jnp.dot(h.astype(BF16), w_ref[...], preferred_element_type=F32)
    q_ref[0] = qkv[:, :Q_DIM].astype(BF16)
    k_ref[0] = qkv[:, Q_DIM:Q_DIM + KV_DIM].astype(BF16)
    v_ref[0] = qkv[:, Q_DIM + KV_DIM:].astype(BF16)


def _qkv(x, s1, sh1, w):
    b, seq, d = x.shape
    ts = TOKEN_TILE
    vec = pl.BlockSpec((1, 1, d), lambda bi, i: (bi, 0, 0))
    tok = lambda n: pl.BlockSpec((1, ts, n), lambda bi, i: (bi, i, 0))
    return pl.pallas_call(
        _qkv_kernel,
        grid=(b, seq // ts),
        in_specs=[tok(d), vec, vec, pl.BlockSpec(w.shape, lambda bi, i: (0, 0))],
        out_specs=[tok(Q_DIM), tok(KV_DIM), tok(KV_DIM)],
        out_shape=[jax.ShapeDtypeStruct((b, seq, Q_DIM), BF16),
                   jax.ShapeDtypeStruct((b, seq, KV_DIM), BF16),
                   jax.ShapeDtypeStruct((b, seq, KV_DIM), BF16)],
        compiler_params=_params(("arbitrary", "arbitrary")),
        name="attn_qkv",
    )(x, s1, sh1, w)


def _attn_kernel(sink_ref, q_ref, kp_ref, kc_ref, kn_ref, vp_ref, vc_ref, vn_ref, x_ref, g_ref,
                 wo_ref, lg_ref, lb_ref, o_ref, att_ref, *, seq, tq, alpha):
    i = pl.program_id(1)
    kcat = jnp.concatenate([kp_ref[0], kc_ref[0], kn_ref[0]], axis=0)
    vcat = jnp.concatenate([vp_ref[0], vc_ref[0], vn_ref[0]], axis=0)
    qi = lax.broadcasted_iota(jnp.int32, (BLOCK, 3 * BLOCK), 0)
    ki = lax.broadcasted_iota(jnp.int32, (BLOCK, 3 * BLOCK), 1)
    rel = jnp.abs(ki - BLOCK - qi)
    relf = rel.astype(F32)
    scale = HEAD_DIM ** -0.5
    slopes = _alibi_slopes()
    for j in range(tq // BLOCK):
        kpos = i * tq + (j - 1) * BLOCK + ki
        valid = (rel <= WINDOW) & (kpos >= 0) & (kpos < seq)
        rows = slice(j * BLOCK, (j + 1) * BLOCK)
        for hd in range(N_HEADS):
            hk = hd // GQA_GROUP
            qh = q_ref[0, rows, hd * HEAD_DIM:(hd + 1) * HEAD_DIM]
            kh = kcat[j * BLOCK:(j + 3) * BLOCK, hk * HEAD_DIM:(hk + 1) * HEAD_DIM]
            vh = vcat[j * BLOCK:(j + 3) * BLOCK, hk * HEAD_DIM:(hk + 1) * HEAD_DIM]
            s = lax.dot_general(qh, kh, (((1,), (1,)), ((), ())), preferred_element_type=F32)
            s = jnp.where(valid, s * scale - slopes[hd] * relf, NEG)
            sink = sink_ref[hd]
            m = jnp.maximum(jnp.max(s, axis=-1, keepdims=True), sink)
            p = jnp.exp(s - m)
            p = p / (jnp.sum(p, axis=-1, keepdims=True) + jnp.exp(sink - m))
            o = jnp.dot(p.astype(BF16), vh, preferred_element_type=F32)
            att_ref[rows, hd * HEAD_DIM:(hd + 1) * HEAD_DIM] = o.astype(BF16)
    f = jnp.dot(att_ref[...], wo_ref[...], preferred_element_type=F32)
    y = alpha * x_ref[0] + g_ref[0] * f
    o_ref[0] = _layer_norm(y, lg_ref[...], lb_ref[...])


def _attn_layer(x, q, k, v, g1, sink, wo, lg, lb, alpha):
    b, seq, d = x.shape
    tq = TOKEN_TILE
    per = tq // BLOCK
    nb = seq // BLOCK
    vec = pl.BlockSpec((1, 1, d), lambda bi, i: (bi, 0, 0))
    row = pl.BlockSpec((1, d), lambda bi, i: (0, 0))
    prev = pl.BlockSpec((1, BLOCK, KV_DIM), lambda bi, i: (bi, jnp.maximum(i * per - 1, 0), 0))
    cur = pl.BlockSpec((1, tq, KV_DIM), lambda bi, i: (bi, i, 0))
    nxt = pl.BlockSpec((1, BLOCK, KV_DIM),
                       lambda bi, i: (bi, jnp.minimum((i + 1) * per, nb - 1), 0))
    return pl.pallas_call(
        functools.partial(_attn_kernel, seq=seq, tq=tq, alpha=alpha),
        grid=(b, seq // tq),
        in_specs=[
            pl.BlockSpec(memory_space=pltpu.SMEM),
            pl.BlockSpec((1, tq, Q_DIM), lambda bi, i: (bi, i, 0)),
            prev, cur, nxt, prev, cur, nxt,
            pl.BlockSpec((1, tq, d), lambda bi, i: (bi, i, 0)),
            vec,
            pl.BlockSpec((d, d), lambda bi, i: (0, 0)),
            row, row,
        ],
        out_specs=pl.BlockSpec((1, tq, d), lambda bi, i: (bi, i, 0)),
        out_shape=jax.ShapeDtypeStruct((b, seq, d), F32),
        scratch_shapes=[pltpu.VMEM((tq, d), BF16)],
        compiler_params=_params(("arbitrary", "arbitrary")),
        name="attn_layer",
    )(sink, q, k, k, k, v, v, v, x, g1, wo, lg, lb)


def _candidate_tables():
    pairs = [(i, j) for i in range(PEER_TOPK) for j in range(PEER_TOPK)
             if (i + 1) * (j + 1) <= PEER_TOPK]
    e1 = np.zeros((LANES, LANES), np.float32)
    e2 = np.zeros((LANES, LANES), np.float32)
    for lane, (i, j) in enumerate(pairs):
        e1[i, lane] = 1.0
        e2[j, lane] = 1.0
    return e1, e2, len(pairs)


def _top16(s, out_lane, base, vals, idxs, payload=None):
    def rnd(r, carry):
        s, vals, idxs = carry
        m = jnp.max(s, axis=-1, keepdims=True)
        am = jnp.min(jnp.where(s == m, out_lane, LANES), axis=-1, keepdims=True)
        hit = out_lane == am
        if payload is None:
            sel = am
        else:
            sel = jnp.max(jnp.where(hit, payload, -1), axis=-1, keepdims=True)
        dst = out_lane == base + r
        return jnp.where(hit, -jnp.inf, s), jnp.where(dst, m, vals), jnp.where(dst, sel, idxs)

    _, vals, idxs = lax.fori_loop(0, PEER_TOPK, rnd, (s, vals, idxs))
    return vals, idxs


def _route_kernel(x_ref, s_ref, sh_ref, wq_ref, keys_ref, e1_ref, e2_ref, eidx_ref, gate_ref,
                  q_ref, *, tr, n_cand):
    h = x_ref[0] * (1.0 + s_ref[0]) + sh_ref[0]
    q = jnp.dot(h.astype(BF16), wq_ref[...], preferred_element_type=F32)
    for hc in range(2 * PEER_HEADS):
        q_ref[hc] = q[:, hc * N_KEYS:(hc + 1) * N_KEYS]
    out_lane = lax.broadcasted_iota(jnp.int32, (tr, LANES), 1)
    zf = jnp.zeros((tr, LANES), F32)
    zi = jnp.zeros((tr, LANES), jnp.int32)
    hi = lax.Precision.HIGHEST

    def head(hh, carry):
        best, eidx, top = carry
        sub = []
        for c in range(2):
            sc = lax.dot_general(q_ref[hh * 2 + c].astype(BF16), keys_ref[hh, c],
                                 (((1,), (1,)), ((), ())), preferred_element_type=F32)
            v16, i16 = _top16(sc, out_lane, 0, zf, zi)
            sub.append((v16, i16.astype(F32).astype(BF16)))
        e1 = e1_ref[...]
        e2 = e2_ref[...]
        cand = (jnp.dot(sub[0][0], e1, precision=hi, preferred_element_type=F32)
                + jnp.dot(sub[1][0], e2, precision=hi, preferred_element_type=F32))
        cand = jnp.where(out_lane < n_cand, cand, -jnp.inf)
        cidx = (jnp.dot(sub[0][1], e1.astype(BF16), preferred_element_type=F32) * float(N_KEYS)
                + jnp.dot(sub[1][1], e2.astype(BF16), preferred_element_type=F32)
                ).astype(jnp.int32)
        best, eidx = _top16(cand, out_lane, hh * PEER_TOPK, best, eidx, payload=cidx)
        head_top = jnp.max(cand, axis=-1, keepdims=True)
        top = jnp.where(out_lane // PEER_TOPK == hh, head_top, top)
        return best, eidx, top

    best, eidx, top = lax.fori_loop(0, PEER_HEADS, head, (zf, zi, zf))
    ex = jnp.exp(best - top)
    gr = lax.broadcasted_iota(jnp.int32, (LANES, LANES), 0) // PEER_TOPK
    gc = lax.broadcasted_iota(jnp.int32, (LANES, LANES), 1) // PEER_TOPK
    seg = (gr == gc).astype(F32)
    den = jnp.dot(ex, seg, precision=hi, preferred_element_type=F32)
    eidx_ref[0] = eidx
    gate_ref[0] = (ex / den).T


def _route(x, s2, sh2, wq, keys):
    b, seq, d = x.shape
    tr = ROUTE_TILE
    e1, e2, n_cand = _candidate_tables()
    vec = pl.BlockSpec((1, 1, d), lambda bi, i: (bi, 0, 0))
    sq = pl.BlockSpec((LANES, LANES), lambda bi, i: (0, 0))
    return pl.pallas_call(
        functools.partial(_route_kernel, tr=tr, n_cand=n_cand),
        grid=(b, seq // tr),
        in_specs=[
            pl.BlockSpec((1, tr, d), lambda bi, i: (bi, i, 0)),
            vec, vec,
            pl.BlockSpec(wq.shape, lambda bi, i: (0, 0)),
            pl.BlockSpec(keys.shape, lambda bi, i: (0, 0, 0, 0)),
            sq, sq,
        ],
        out_specs=[pl.BlockSpec((1, tr, PEER_SEL), lambda bi, i: (bi, i, 0)),
                   pl.BlockSpec((1, PEER_SEL, tr), lambda bi, i: (bi, 0, i))],
        out_shape=[jax.ShapeDtypeStruct((b, seq, PEER_SEL), jnp.int32),
                   jax.ShapeDtypeStruct((b, PEER_SEL, seq), F32)],
        scratch_shapes=[pltpu.VMEM((2 * PEER_HEADS, tr, N_KEYS), F32)],
        compiler_params=_params(("arbitrary", "arbitrary")),
        name="peer_route",
    )(x, s2, sh2, wq, keys, jnp.asarray(e1), jnp.asarray(e2))


def _expert_kernel(idx_ref, gate_ref, x_ref, s_ref, sh_ref, uv_ref, o_ref, *scratch, tb, nbuf):
    bufs = scratch[:nbuf]
    sem = scratch[nbuf]
    wb_ref = scratch[nbuf + 1]
    s2 = 1.0 + s_ref[0]
    sh = sh_ref[0]
    tok_lane = lax.broadcasted_iota(jnp.int32, (PEER_SEL, tb), 1)

    def row_copy(t, e, slot):
        return pltpu.make_async_copy(uv_ref.at[idx_ref[t, e]], bufs[slot].at[e], sem.at[slot])

    def issue(t, slot):
        for e in range(PEER_SEL):
            row_copy(t, e, slot).start(priority=e % 2)

    def wait(slot):
        pltpu.make_async_copy(uv_ref.at[pl.ds(0, PEER_SEL)], bufs[slot], sem.at[slot]).wait()

    def compute(t, slot):
        buf = bufs[slot]
        xt = x_ref[t] * s2 + sh
        parts = []
        for c in range(PEER_SEL // SUBLANES):
            u8 = buf[c * SUBLANES:(c + 1) * SUBLANES, 0:ROW_TILES, :]
            parts.append(jnp.sum(u8 * xt[None], axis=1))
        a = jnp.sum(jnp.concatenate(parts, axis=0), axis=-1, keepdims=True)
        act = 0.5 * a * (1.0 + lax.erf(a * np.float32(np.sqrt(0.5))))
        gcol = jnp.sum(jnp.where(tok_lane == t, gate_ref[0], 0.0), axis=-1, keepdims=True)
        wb_ref[...] = jnp.broadcast_to(act * gcol, (PEER_SEL, LANES))
        accs = [None] * 4
        for e in range(PEER_SEL):
            term = wb_ref[e:e + 1, :] * buf[e, ROW_TILES:2 * ROW_TILES, :]
            accs[e % 4] = term if accs[e % 4] is None else accs[e % 4] + term
        o_ref[t] = (accs[0] + accs[1]) + (accs[2] + accs[3])

    for t in range(nbuf - 1):
        issue(t, t)

    def group(gi, carry):
        for k in range(nbuf):
            t = gi * nbuf + k
            issue(t + nbuf - 1, (k + nbuf - 1) % nbuf)
            wait(k)
            compute(t, k)
        return carry

    n_groups = tb // nbuf
    lax.fori_loop(0, n_groups - 1, group, 0)
    for k in range(nbuf):
        t = (n_groups - 1) * nbuf + k
        if t + nbuf - 1 < tb:
            issue(t + nbuf - 1, (k + nbuf - 1) % nbuf)
        wait(k)
        compute(t, k)


def _experts(eidx, gate_t, x3, s2, sh2, uv, seq):
    t_all = x3.shape[0]
    tb, nbuf = PEER_TB, PEER_NBUF
    per_seq = seq // tb
    tok = pl.BlockSpec((tb, ROW_TILES, LANES), lambda i: (i, 0, 0))
    vec = pl.BlockSpec((1, ROW_TILES, LANES), lambda i: (i // per_seq, 0, 0))
    return pl.pallas_call(
        functools.partial(_expert_kernel, tb=tb, nbuf=nbuf),
        grid=(t_all // tb,),
        in_specs=[
            pl.BlockSpec((tb, PEER_SEL), lambda i: (i, 0), memory_space=pltpu.SMEM),
            pl.BlockSpec((1, PEER_SEL, tb), lambda i: (i // per_seq, 0, i % per_seq)),
            tok, vec, vec,
            pl.BlockSpec(memory_space=pl.ANY),
        ],
        out_specs=tok,
        out_shape=jax.ShapeDtypeStruct(x3.shape, F32),
        scratch_shapes=(
            [pltpu.VMEM((PEER_SEL, 2 * ROW_TILES, LANES), F32) for _ in range(nbuf)]
            + [pltpu.SemaphoreType.DMA((nbuf,)), pltpu.VMEM((PEER_SEL, LANES), F32)]),
        compiler_params=_params(("arbitrary",)),
        name="peer_experts",
    )(eidx, gate_t, x3, s2, sh2, uv)


def _res_ln_kernel(x_ref, f_ref, g_ref, lg_ref, lb_ref, o_ref, *, alpha):
    y = alpha * x_ref[0] + g_ref[0] * f_ref[0]
    o_ref[0] = _layer_norm(y, lg_ref[...], lb_ref[...])


def _res_ln(x, f, g, lg, lb, alpha):
    b, seq, d = x.shape
    ts = TOKEN_TILE
    tok = pl.BlockSpec((1, ts, d), lambda bi, i: (bi, i, 0))
    row = pl.BlockSpec((1, d), lambda bi, i: (0, 0))
    return pl.pallas_call(
        functools.partial(_res_ln_kernel, alpha=alpha),
        grid=(b, seq // ts),
        in_specs=[tok, tok, pl.BlockSpec((1, 1, d), lambda bi, i: (bi, 0, 0)), row, row],
        out_specs=tok,
        out_shape=jax.ShapeDtypeStruct(x.shape, F32),
        compiler_params=_params(("arbitrary", "arbitrary")),
        name="res_ln",
    )(x, f, g, lg, lb)


def _trunk(x, mod, w):
    b, seq, d = x.shape
    depth = mod.shape[0]
    alpha = float((2 * depth) ** 0.25)
    for i in range(depth):
        sh1, s1, g1, sh2, s2, g2 = [m.reshape(b, 1, d) for m in jnp.split(mod[i], 6, axis=-1)]
        lw = w[i]
        if i % 2 == 0:
            x = _pool_layer(x, s1, sh1, g1, lw["pool_w"], lw["pool_scale"], lw["wo"],
                            lw["ln1_g"], lw["ln1_b"], alpha)
        else:
            q, k, v = _qkv(x, s1, sh1, lw["wqkv"])
            x = _attn_layer(x, q, k, v, g1, lw["sink"], lw["wo"], lw["ln1_g"], lw["ln1_b"], alpha)
        eidx, gate_t = _route(x, s2, sh2, lw["peer_wq"], lw["peer_keys"])
        f3 = _experts(eidx.reshape(b * seq, PEER_SEL), gate_t,
                      x.reshape(b * seq, ROW_TILES, LANES),
                      s2.reshape(b, ROW_TILES, LANES), sh2.reshape(b, ROW_TILES, LANES),
                      lw["uv"], seq)
        x = _res_ln(x, f3.reshape(b, seq, d), g2, lw["ln2_g"], lw["ln2_b"], alpha)
    return x


def kernel(x_prompt, x_sample, c_prompt, c_sample, ada_w, ada_b, ln1_g, ln1_b, ln2_g, ln2_b,
           pool_w, pool_scale, pool_wo, attn_wqkv, attn_sink, attn_wo,
           peer_wq, peer_keys, peer_u, peer_v):
    depth, d = ln1_g.shape
    n_exp = peer_u.shape[1]
    bp, bs = c_prompt.shape[0], c_sample.shape[0]
    m_pad = -(bp + bs) % SUBLANES
    c_all = jnp.concatenate([c_prompt, c_sample, jnp.zeros((m_pad, d), F32)], axis=0)
    mod = _ada(c_all, ada_w, ada_b)

    layers = []
    for i in range(depth):
        j = i // 2
        lw = {
            "ln1_g": ln1_g[i].reshape(1, d), "ln1_b": ln1_b[i].reshape(1, d),
            "ln2_g": ln2_g[i].reshape(1, d), "ln2_b": ln2_b[i].reshape(1, d),
            "peer_wq": peer_wq[i].astype(BF16),
            "peer_keys": peer_keys[i].astype(BF16),
            "uv": jnp.concatenate([peer_u[i], peer_v[i]], axis=-1).reshape(
                n_exp, 2 * ROW_TILES, LANES),
        }
        if i % 2 == 0:
            lw.update(pool_w=pool_w[j].astype(BF16), pool_scale=pool_scale[j].reshape(1, d),
                      wo=pool_wo[j].astype(BF16))
        else:
            lw.update(wqkv=attn_wqkv[j].astype(BF16), sink=attn_sink[j],
                      wo=attn_wo[j].astype(BF16))
        layers.append(lw)

    y_prompt = _trunk(x_prompt, mod[:, :bp], layers)
    y_sample = _trunk(x_sample, mod[:, bp:bp + bs], layers)
    return (y_prompt, y_sample)
```

```python
import functools

import numpy as np
import jax
import jax.numpy as jnp
from jax import lax
from jax.experimental import pallas as pl
from jax.experimental.pallas import tpu as pltpu

F32 = jnp.float32
BF16 = jnp.bfloat16

D_MODEL = 2048
POOL_WINDOWS = (2, 4, 8, 16)
POOL_CH = D_MODEL // len(POOL_WINDOWS)
POOL_HALO = 8
HEAD_DIM = 128
N_HEADS = D_MODEL // HEAD_DIM
N_KV_HEADS = 4
GQA_GROUP = N_HEADS // N_KV_HEADS
WINDOW = 128
BLOCK = 128
Q_DIM = N_HEADS * HEAD_DIM
KV_DIM = N_KV_HEADS * HEAD_DIM
N_KEYS = 128
PEER_HEADS = 8
PEER_TOPK = 16
PEER_SEL = PEER_HEADS * PEER_TOPK
LN_EPS = 1e-5
NEG = -1e30

LANES = 128
SUBLANES = 8
ROW_TILES = D_MODEL // LANES
VMEM_LIMIT = 52 * 1024 * 1024

ADA_TN = 1024
TOKEN_TILE = 256
ROUTE_TILE = 256
PEER_TB = 128
PEER_NBUF = 4


def _alibi_slopes():
    return [2.0 ** (-8.0 * (h + 1) / N_HEADS) for h in range(N_HEADS)]


def _params(sem):
    return pltpu.CompilerParams(dimension_semantics=sem, vmem_limit_bytes=VMEM_LIMIT)


def _layer_norm(y, g, b):
    mu = jnp.mean(y, axis=-1, keepdims=True)
    d = y - mu
    var = jnp.mean(d * d, axis=-1, keepdims=True)
    return d * lax.rsqrt(var + LN_EPS) * g + b


def _ada_kernel(c_ref, w_ref, b_ref, o_ref):
    c = c_ref[...]
    sc = c * jax.nn.sigmoid(c)
    o_ref[0] = jnp.dot(sc.astype(BF16), w_ref[0].astype(BF16),
                       preferred_element_type=F32) + b_ref[0]


def _ada(c_all, ada_w, ada_b):
    depth, d, n = ada_w.shape
    m = c_all.shape[0]
    return pl.pallas_call(
        _ada_kernel,
        grid=(depth, n // ADA_TN),
        in_specs=[
            pl.BlockSpec((m, d), lambda l, j: (0, 0)),
            pl.BlockSpec((1, d, ADA_TN), lambda l, j: (l, 0, j)),
            pl.BlockSpec((1, 1, ADA_TN), lambda l, j: (l, 0, j)),
        ],
        out_specs=pl.BlockSpec((1, m, ADA_TN), lambda l, j: (l, 0, j)),
        out_shape=jax.ShapeDtypeStruct((depth, m, n), F32),
        compiler_params=_params(("arbitrary", "arbitrary")),
        name="ada_mod",
    )(c_all, ada_w, ada_b.reshape(depth, 1, n))


def _pool_kernel(xp_ref, x_ref, xn_ref, s_ref, sh_ref, g_ref, wg_ref, ps_ref, wo_ref,
                 lg_ref, lb_ref, o_ref, *, seq, ts, alpha):
    i = pl.program_id(1)
    n_i = pl.num_programs(1)
    s1 = 1.0 + s_ref[0]
    sh = sh_ref[0]
    x = x_ref[0]
    h = x * s1 + sh
    hp = jnp.where(i > 0, xp_ref[0] * s1 + sh, 0.0)
    hn = jnp.where(i < n_i - 1, xn_ref[0] * s1 + sh, 0.0)
    ext = jnp.concatenate([hp, h, hn], axis=0)
    n_ext = ts + 2 * POOL_HALO
    pos = i * ts + lax.broadcasted_iota(jnp.int32, (ts, 1), 0)

    def shifted(a, k):
        return pltpu.roll(a, k % n_ext, axis=0)

    f = None
    for g, w in enumerate(POOL_WINDOWS):
        cs = slice(g * POOL_CH, (g + 1) * POOL_CH)
        e = ext[:, cs]
        acc = e + shifted(e, 1)
        half = 1
        while 2 * half < w:
            acc = shifted(acc, half) + shifted(acc, -half)
            half *= 2
        win = acc[POOL_HALO:POOL_HALO + ts]
        cnt = (jnp.minimum(pos + w // 2, seq) - jnp.maximum(pos - w // 2, 0)).astype(F32)
        pooled = win / cnt - h[:, cs]
        mixed = jnp.dot(pooled.astype(BF16), wg_ref[g], preferred_element_type=F32)
        mixed = mixed * ps_ref[:, cs]
        part = jnp.dot(mixed.astype(BF16), wo_ref[cs, :], preferred_element_type=F32)
        f = part if f is None else f + part
    y = alpha * x + g_ref[0] * f
    o_ref[0] = _layer_norm(y, lg_ref[...], lb_ref[...])


def _pool_layer(x, s1, sh1, g1, wg, ps, wo, lg, lb, alpha):
    b, seq, d = x.shape
    ts = TOKEN_TILE
    per = ts // POOL_HALO
    n_halo = seq // POOL_HALO
    vec = pl.BlockSpec((1, 1, d), lambda bi, i: (bi, 0, 0))
    row = pl.BlockSpec((1, d), lambda bi, i: (0, 0))
    return pl.pallas_call(
        functools.partial(_pool_kernel, seq=seq, ts=ts, alpha=alpha),
        grid=(b, seq // ts),
        in_specs=[
            pl.BlockSpec((1, POOL_HALO, d), lambda bi, i: (bi, jnp.maximum(i * per - 1, 0), 0)),
            pl.BlockSpec((1, ts, d), lambda bi, i: (bi, i, 0)),
            pl.BlockSpec((1, POOL_HALO, d),
                         lambda bi, i: (bi, jnp.minimum((i + 1) * per, n_halo - 1), 0)),
            vec, vec, vec,
            pl.BlockSpec(wg.shape, lambda bi, i: (0, 0, 0)),
            row,
            pl.BlockSpec((d, d), lambda bi, i: (0, 0)),
            row, row,
        ],
        out_specs=pl.BlockSpec((1, ts, d), lambda bi, i: (bi, i, 0)),
        out_shape=jax.ShapeDtypeStruct((b, seq, d), F32),
        compiler_params=_params(("arbitrary", "arbitrary")),
        name="pool_layer",
    )(x, x, x, s1, sh1, g1, wg, ps, wo, lg, lb)


def _qkv_kernel(x_ref, s_ref, sh_ref, w_ref, q_ref, k_ref, v_ref):
    h = x_ref[0] * (1.0 + s_ref[0]) + sh_ref[0]
    qkv = jnp.dot(h.astype(BF16), w_ref[...], preferred_element_type=F32)
    q_ref[0] = qkv[:, :Q_DIM].astype(BF16)
    k_ref[0] = qkv[:, Q_DIM:Q_DIM + KV_DIM].astype(BF16)
    v_ref[0] = qkv[:, Q_DIM + KV_DIM:].astype(BF16)


def _qkv(x, s1, sh1, w):
    b, seq, d = x.shape
    ts = TOKEN_TILE
    vec = pl.BlockSpec((1, 1, d), lambda bi, i: (bi, 0, 0))
    tok = lambda n: pl.BlockSpec((1, ts, n), lambda bi, i: (bi, i, 0))
    return pl.pallas_call(
        _qkv_kernel,
        grid=(b, seq // ts),
        in_specs=[tok(d), vec, vec, pl.BlockSpec(w.shape, lambda bi, i: (0, 0))],
        out_specs=[tok(Q_DIM), tok(KV_DIM), tok(KV_DIM)],
        out_shape=[jax.ShapeDtypeStruct((b, seq, Q_DIM), BF16),
                   jax.ShapeDtypeStruct((b, seq, KV_DIM), BF16),
                   jax.ShapeDtypeStruct((b, seq, KV_DIM), BF16)],
        compiler_params=_params(("arbitrary", "arbitrary")),
        name="attn_qkv",
    )(x, s1, sh1, w)


def _attn_kernel(sink_ref, q_ref, kp_ref, kc_ref, kn_ref, vp_ref, vc_ref, vn_ref, x_ref, g_ref,
                 wo_ref, lg_ref, lb_ref, o_ref, att_ref, *, seq, tq, alpha):
    i = pl.program_id(1)
    kcat = jnp.concatenate([kp_ref[0], kc_ref[0], kn_ref[0]], axis=0)
    vcat = jnp.concatenate([vp_ref[0], vc_ref[0], vn_ref[0]], axis=0)
    qi = lax.broadcasted_iota(jnp.int32, (BLOCK, 3 * BLOCK), 0)
    ki = lax.broadcasted_iota(jnp.int32, (BLOCK, 3 * BLOCK), 1)
    rel = jnp.abs(ki - BLOCK - qi)
    relf = rel.astype(F32)
    scale = HEAD_DIM ** -0.5
    slopes = _alibi_slopes()
    for j in range(tq // BLOCK):
        kpos = i * tq + (j - 1) * BLOCK + ki
        valid = (rel <= WINDOW) & (kpos >= 0) & (kpos < seq)
        rows = slice(j * BLOCK, (j + 1) * BLOCK)
        for hd in range(N_HEADS):
            hk = hd // GQA_GROUP
            qh = q_ref[0, rows, hd * HEAD_DIM:(hd + 1) * HEAD_DIM]
            kh = kcat[j * BLOCK:(j + 3) * BLOCK, hk * HEAD_DIM:(hk + 1) * HEAD_DIM]
            vh = vcat[j * BLOCK:(j + 3) * BLOCK, hk * HEAD_DIM:(hk + 1) * HEAD_DIM]
            s = lax.dot_general(qh, kh, (((1,), (1,)), ((), ())), preferred_element_type=F32)
            s = jnp.where(valid, s * scale - slopes[hd] * relf, NEG)
            sink = sink_ref[hd]
            m = jnp.maximum(jnp.max(s, axis=-1, keepdims=True), sink)
            p = jnp.exp(s - m)
            p = p / (jnp.sum(p, axis=-1, keepdims=True) + jnp.exp(sink - m))
            o = jnp.dot(p.astype(BF16), vh, preferred_element_type=F32)
            att_ref[rows, hd * HEAD_DIM:(hd + 1) * HEAD_DIM] = o.astype(BF16)
    f = jnp.dot(att_ref[...], wo_ref[...], preferred_element_type=F32)
    y = alpha * x_ref[0] + g_ref[0] * f
    o_ref[0] = _layer_norm(y, lg_ref[...], lb_ref[...])


def _attn_layer(x, q, k, v, g1, sink, wo, lg, lb, alpha):
    b, seq, d = x.shape
    tq = TOKEN_TILE
    per = tq // BLOCK
    nb = seq // BLOCK
    vec = pl.BlockSpec((1, 1, d), lambda bi, i: (bi, 0, 0))
    row = pl.BlockSpec((1, d), lambda bi, i: (0, 0))
    prev = pl.BlockSpec((1, BLOCK, KV_DIM), lambda bi, i: (bi, jnp.maximum(i * per - 1, 0), 0))
    cur = pl.BlockSpec((1, tq, KV_DIM), lambda bi, i: (bi, i, 0))
    nxt = pl.BlockSpec((1, BLOCK, KV_DIM),
                       lambda bi, i: (bi, jnp.minimum((i + 1) * per, nb - 1), 0))
    return pl.pallas_call(
        functools.partial(_attn_kernel, seq=seq, tq=tq, alpha=alpha),
        grid=(b, seq // tq),
        in_specs=[
            pl.BlockSpec(memory_space=pltpu.SMEM),
            pl.BlockSpec((1, tq, Q_DIM), lambda bi, i: (bi, i, 0)),
            prev, cur, nxt, prev, cur, nxt,
            pl.BlockSpec((1, tq, d), lambda bi, i: (bi, i, 0)),
            vec,
            pl.BlockSpec((d, d), lambda bi, i: (0, 0)),
            row, row,
        ],
        out_specs=pl.BlockSpec((1, tq, d), lambda bi, i: (bi, i, 0)),
        out_shape=jax.ShapeDtypeStruct((b, seq, d), F32),
        scratch_shapes=[pltpu.VMEM((tq, d), BF16)],
        compiler_params=_params(("arbitrary", "arbitrary")),
        name="attn_layer",
    )(sink, q, k, k, k, v, v, v, x, g1, wo, lg, lb)


def _top16_rows(scores, payloads):
    slot = lax.broadcasted_iota(jnp.int32, (PEER_TOPK, LANES), 0)
    rows = [lax.broadcasted_iota(jnp.int32, s.shape, 0).astype(F32) for s in scores]
    zero = jnp.zeros((PEER_TOPK, LANES), F32)

    def rnd(r, carry):
        out = []
        for (s, v, p), row, pay in zip(carry, rows, payloads):
            m = jnp.max(s, axis=0, keepdims=True)
            am = jnp.min(jnp.where(s == m, row, float(s.shape[0])), axis=0, keepdims=True)
            hit = row == am
            sel = am if pay is None else jnp.max(jnp.where(hit, pay, -1.0), axis=0, keepdims=True)
            out.append((jnp.where(hit, -jnp.inf, s), jnp.where(slot == r, m, v),
                        jnp.where(slot == r, sel, p)))
        return tuple(out)

    done = lax.fori_loop(0, PEER_TOPK, rnd, tuple((s, zero, zero) for s in scores))
    return [(v, p) for _, v, p in done]


def _route_kernel(x_ref, s_ref, sh_ref, wqt_ref, keys_ref, eidx_ref, gate_ref,
                  qt_ref, ei_ref, g_ref, *, tr):
    n_lt = tr // LANES
    h = x_ref[0] * (1.0 + s_ref[0]) + sh_ref[0]
    qt = lax.dot_general(wqt_ref[...], h.astype(BF16), (((1,), (1,)), ((), ())),
                         preferred_element_type=F32)
    for lt in range(n_lt):
        for hc in range(2 * PEER_HEADS):
            qt_ref[lt, hc] = qt[hc * N_KEYS:(hc + 1) * N_KEYS,
                                lt * LANES:(lt + 1) * LANES].astype(BF16)
    row8 = lax.broadcasted_iota(jnp.int32, (SUBLANES, LANES), 0)

    def job(hh, lt):
        sc = [jnp.dot(keys_ref[hh, c], qt_ref[lt, hh * 2 + c], preferred_element_type=F32)
              for c in range(2)]
        (v1, i1), (v2, i2) = _top16_rows(sc, [None, None])
        cand, cidx = [], []
        for i in range(PEER_TOPK):
            n_j = PEER_TOPK // (i + 1)
            a = v1[i:i + 1, :]
            ai = i1[i:i + 1, :] * float(N_KEYS)
            for j0 in range(0, n_j, SUBLANES):
                ok = row8 < (n_j - j0)
                cand.append(jnp.where(ok, a + v2[j0:j0 + SUBLANES, :], -jnp.inf))
                cidx.append(ai + i2[j0:j0 + SUBLANES, :])
        cand = jnp.concatenate(cand, axis=0)
        cidx = jnp.concatenate(cidx, axis=0)
        ((best, eid),) = _top16_rows([cand], [cidx])
        ex = jnp.exp(best - best[0:1, :])
        gate = ex / jnp.sum(ex, axis=0, keepdims=True)
        dst = pl.ds(pl.multiple_of(hh * PEER_TOPK, PEER_TOPK), PEER_TOPK)
        ei_ref[lt, dst, :] = eid.astype(jnp.int32)
        g_ref[lt, dst, :] = gate

    def head(hh, carry):
        for lt in range(n_lt):
            job(hh, lt)
        return carry

    lax.fori_loop(0, PEER_HEADS, head, 0)
    for lt in range(n_lt):
        eidx_ref[0, :, lt * LANES:(lt + 1) * LANES] = ei_ref[lt]
        gate_ref[0, :, lt * LANES:(lt + 1) * LANES] = g_ref[lt]


def _route(x, s2, sh2, wqt, keys):
    b, seq, d = x.shape
    tr = ROUTE_TILE
    n_lt = tr // LANES
    vec = pl.BlockSpec((1, 1, d), lambda bi, i: (bi, 0, 0))
    out = pl.BlockSpec((1, PEER_SEL, tr), lambda bi, i: (bi, 0, i))
    return pl.pallas_call(
        functools.partial(_route_kernel, tr=tr),
        grid=(b, seq // tr),
        in_specs=[
            pl.BlockSpec((1, tr, d), lambda bi, i: (bi, i, 0)),
            vec, vec,
            pl.BlockSpec(wqt.shape, lambda bi, i: (0, 0)),
            pl.BlockSpec(keys.shape, lambda bi, i: (0, 0, 0, 0)),
        ],
        out_specs=[out, out],
        out_shape=[jax.ShapeDtypeStruct((b, PEER_SEL, seq), jnp.int32),
                   jax.ShapeDtypeStruct((b, PEER_SEL, seq), F32)],
        scratch_shapes=[pltpu.VMEM((n_lt, 2 * PEER_HEADS, N_KEYS, LANES), BF16),
                        pltpu.VMEM((n_lt, PEER_SEL, LANES), jnp.int32),
                        pltpu.VMEM((n_lt, PEER_SEL, LANES), F32)],
        compiler_params=_params(("arbitrary", "arbitrary")),
        name="peer_route",
    )(x, s2, sh2, wqt, keys)


def _expert_kernel(idx_ref, gate_ref, x_ref, s_ref, sh_ref, uv_ref, o_ref, *scratch, tb, nbuf):
    bufs = scratch[:nbuf]
    sem = scratch[nbuf]
    wb_ref = scratch[nbuf + 1]
    s2 = 1.0 + s_ref[0]
    sh = sh_ref[0]
    tok_lane = lax.broadcasted_iota(jnp.int32, (PEER_SEL, tb), 1)

    def row_copy(t, e, slot):
        return pltpu.make_async_copy(uv_ref.at[idx_ref[e, t]], bufs[slot].at[e], sem.at[slot])

    def issue(t, slot):
        for e in range(PEER_SEL):
            row_copy(t, e, slot).start(priority=e % 2)

    def wait(slot):
        pltpu.make_async_copy(uv_ref.at[pl.ds(0, PEER_SEL)], bufs[slot], sem.at[slot]).wait()

    def compute(t, slot):
        buf = bufs[slot]
        xt = x_ref[t] * s2 + sh
        parts = []
        for c in range(PEER_SEL // SUBLANES):
            u8 = buf[c * SUBLANES:(c + 1) * SUBLANES, 0:ROW_TILES, :]
            parts.append(jnp.sum(u8 * xt[None], axis=1))
        a = jnp.sum(jnp.concatenate(parts, axis=0), axis=-1, keepdims=True)
        act = 0.5 * a * (1.0 + lax.erf(a * np.float32(np.sqrt(0.5))))
        gcol = jnp.sum(jnp.where(tok_lane == t, gate_ref[0], 0.0), axis=-1, keepdims=True)
        wb_ref[...] = jnp.broadcast_to(act * gcol, (PEER_SEL, LANES))
        accs = [None] * 4
        for e in range(PEER_SEL):
            term = wb_ref[e:e + 1, :] * buf[e, ROW_TILES:2 * ROW_TILES, :]
            accs[e % 4] = term if accs[e % 4] is None else accs[e % 4] + term
        o_ref[t] = (accs[0] + accs[1]) + (accs[2] + accs[3])

    for t in range(nbuf - 1):
        issue(t, t)

    def group(gi, carry):
        for k in range(nbuf):
            t = gi * nbuf + k
            issue(t + nbuf - 1, (k + nbuf - 1) % nbuf)
            wait(k)
            compute(t, k)
        return carry

    n_groups = tb // nbuf
    lax.fori_loop(0, n_groups - 1, group, 0)
    for k in range(nbuf):
        t = (n_groups - 1) * nbuf + k
        if t + nbuf - 1 < tb:
            issue(t + nbuf - 1, (k + nbuf - 1) % nbuf)
        wait(k)
        compute(t, k)


def _experts(eidx, gate_t, x3, s2, sh2, uv, seq):
    t_all = x3.shape[0]
    tb, nbuf = PEER_TB, PEER_NBUF
    per_seq = seq // tb
    tok = pl.BlockSpec((tb, ROW_TILES, LANES), lambda i: (i, 0, 0))
    vec = pl.BlockSpec((1, ROW_TILES, LANES), lambda i: (i // per_seq, 0, 0))
    return pl.pallas_call(
        functools.partial(_expert_kernel, tb=tb, nbuf=nbuf),
        grid=(t_all // tb,),
        in_specs=[
            pl.BlockSpec((None, PEER_SEL, tb), lambda i: (i // per_seq, 0, i % per_seq),
                         memory_space=pltpu.SMEM),
            pl.BlockSpec((1, PEER_SEL, tb), lambda i: (i // per_seq, 0, i % per_seq)),
            tok, vec, vec,
            pl.BlockSpec(memory_space=pl.ANY),
        ],
        out_specs=tok,
        out_shape=jax.ShapeDtypeStruct(x3.shape, F32),
        scratch_shapes=(
            [pltpu.VMEM((PEER_SEL, 2 * ROW_TILES, LANES), F32) for _ in range(nbuf)]
            + [pltpu.SemaphoreType.DMA((nbuf,)), pltpu.VMEM((PEER_SEL, LANES), F32)]),
        compiler_params=_params(("arbitrary",)),
        name="peer_experts",
    )(eidx, gate_t, x3, s2, sh2, uv)


def _res_ln_kernel(x_ref, f_ref, g_ref, lg_ref, lb_ref, o_ref, *, alpha):
    y = alpha * x_ref[0] + g_ref[0] * f_ref[0]
    o_ref[0] = _layer_norm(y, lg_ref[...], lb_ref[...])


def _res_ln(x, f, g, lg, lb, alpha):
    b, seq, d = x.shape
    ts = TOKEN_TILE
    tok = pl.BlockSpec((1, ts, d), lambda bi, i: (bi, i, 0))
    row = pl.BlockSpec((1, d), lambda bi, i: (0, 0))
    return pl.pallas_call(
        functools.partial(_res_ln_kernel, alpha=alpha),
        grid=(b, seq // ts),
        in_specs=[tok, tok, pl.BlockSpec((1, 1, d), lambda bi, i: (bi, 0, 0)), row, row],
        out_specs=tok,
        out_shape=jax.ShapeDtypeStruct(x.shape, F32),
        compiler_params=_params(("arbitrary", "arbitrary")),
        name="res_ln",
    )(x, f, g, lg, lb)


def _trunk(x, mod, w):
    b, seq, d = x.shape
    depth = mod.shape[0]
    alpha = float((2 * depth) ** 0.25)
    for i in range(depth):
        sh1, s1, g1, sh2, s2, g2 = [m.reshape(b, 1, d) for m in jnp.split(mod[i], 6, axis=-1)]
        lw = w[i]
        if i % 2 == 0:
            x = _pool_layer(x, s1, sh1, g1, lw["pool_w"], lw["pool_scale"], lw["wo"],
                            lw["ln1_g"], lw["ln1_b"], alpha)
        else:
            q, k, v = _qkv(x, s1, sh1, lw["wqkv"])
            x = _attn_layer(x, q, k, v, g1, lw["sink"], lw["wo"], lw["ln1_g"], lw["ln1_b"], alpha)
        eidx, gate_t = _route(x, s2, sh2, lw["peer_wqt"], lw["peer_keys"])
        f3 = _experts(eidx, gate_t,
                      x.reshape(b * seq, ROW_TILES, LANES),
                      s2.reshape(b, ROW_TILES, LANES), sh2.reshape(b, ROW_TILES, LANES),
                      lw["uv"], seq)
        x = _res_ln(x, f3.reshape(b, seq, d), g2, lw["ln2_g"], lw["ln2_b"], alpha)
    return x


def kernel(x_prompt, x_sample, c_prompt, c_sample, ada_w, ada_b, ln1_g, ln1_b, ln2_g, ln2_b,
           pool_w, pool_scale, pool_wo, attn_wqkv, attn_sink, attn_wo,
           peer_wq, peer_keys, peer_u, peer_v):
    depth, d = ln1_g.shape
    n_exp = peer_u.shape[1]
    bp, bs = c_prompt.shape[0], c_sample.shape[0]
    m_pad = -(bp + bs) % SUBLANES
    c_all = jnp.concatenate([c_prompt, c_sample, jnp.zeros((m_pad, d), F32)], axis=0)
    mod = _ada(c_all, ada_w, ada_b)

    layers = []
    for i in range(depth):
        j = i // 2
        lw = {
            "ln1_g": ln1_g[i].reshape(1, d), "ln1_b": ln1_b[i].reshape(1, d),
            "ln2_g": ln2_g[i].reshape(1, d), "ln2_b": ln2_b[i].reshape(1, d),
            "peer_wqt": peer_wq[i].T.astype(BF16),
            "peer_keys": peer_keys[i].astype(BF16),
            "uv": jnp.concatenate([peer_u[i], peer_v[i]], axis=-1).reshape(
                n_exp, 2 * ROW_TILES, LANES),
        }
        if i % 2 == 0:
            lw.update(pool_w=pool_w[j].astype(BF16), pool_scale=pool_scale[j].reshape(1, d),
                      wo=pool_wo[j].astype(BF16))
        else:
            lw.update(wqkv=attn_wqkv[j].astype(BF16), sink=attn_sink[j],
                      wo=attn_wo[j].astype(BF16))
        layers.append(lw)

    y_prompt = _trunk(x_prompt, mod[:, :bp], layers)
    y_sample = _trunk(x_sample, mod[:, bp:bp + bs], layers)
    return (y_prompt, y_sample)
```

```python
import functools

import numpy as np
import jax
import jax.numpy as jnp
from jax import lax
from jax.experimental import pallas as pl
from jax.experimental.pallas import tpu as pltpu

F32 = jnp.float32
BF16 = jnp.bfloat16

D_MODEL = 2048
POOL_WINDOWS = (2, 4, 8, 16)
POOL_CH = D_MODEL // len(POOL_WINDOWS)
POOL_HALO = 8
HEAD_DIM = 128
N_HEADS = D_MODEL // HEAD_DIM
N_KV_HEADS = 4
GQA_GROUP = N_HEADS // N_KV_HEADS
WINDOW = 128
BLOCK = 128
Q_DIM = N_HEADS * HEAD_DIM
KV_DIM = N_KV_HEADS * HEAD_DIM
N_KEYS = 128
PEER_HEADS = 8
PEER_TOPK = 16
PEER_SEL = PEER_HEADS * PEER_TOPK
LN_EPS = 1e-5
NEG = -1e30

LANES = 128
SUBLANES = 8
ROW_TILES = D_MODEL // LANES
VMEM_LIMIT = 52 * 1024 * 1024

ADA_TN = 1024
TOKEN_TILE = 256
ROUTE_TILE = 256
PEER_TB = 128
PEER_NBUF = 8


def _alibi_slopes():
    return [2.0 ** (-8.0 * (h + 1) / N_HEADS) for h in range(N_HEADS)]


def _params(sem):
    return pltpu.CompilerParams(dimension_semantics=sem, vmem_limit_bytes=VMEM_LIMIT)


def _layer_norm(y, g, b):
    mu = jnp.mean(y, axis=-1, keepdims=True)
    d = y - mu
    var = jnp.mean(d * d, axis=-1, keepdims=True)
    return d * lax.rsqrt(var + LN_EPS) * g + b


def _ada_kernel(c_ref, w_ref, b_ref, o_ref):
    c = c_ref[...]
    sc = c * jax.nn.sigmoid(c)
    o_ref[0] = jnp.dot(sc.astype(BF16), w_ref[0].astype(BF16),
                       preferred_element_type=F32) + b_ref[0]


def _ada(c_all, ada_w, ada_b):
    depth, d, n = ada_w.shape
    m = c_all.shape[0]
    return pl.pallas_call(
        _ada_kernel,
        grid=(depth, n // ADA_TN),
        in_specs=[
            pl.BlockSpec((m, d), lambda l, j: (0, 0)),
            pl.BlockSpec((1, d, ADA_TN), lambda l, j: (l, 0, j)),
            pl.BlockSpec((1, 1, ADA_TN), lambda l, j: (l, 0, j)),
        ],
        out_specs=pl.BlockSpec((1, m, ADA_TN), lambda l, j: (l, 0, j)),
        out_shape=jax.ShapeDtypeStruct((depth, m, n), F32),
        compiler_params=_params(("arbitrary", "arbitrary")),
        name="ada_mod",
    )(c_all, ada_w, ada_b.reshape(depth, 1, n))


def _pool_kernel(xp_ref, x_ref, xn_ref, s_ref, sh_ref, g_ref, wg_ref, ps_ref, wo_ref,
                 lg_ref, lb_ref, o_ref, *, seq, ts, alpha):
    i = pl.program_id(1)
    n_i = pl.num_programs(1)
    s1 = 1.0 + s_ref[0]
    sh = sh_ref[0]
    x = x_ref[0]
    h = x * s1 + sh
    hp = jnp.where(i > 0, xp_ref[0] * s1 + sh, 0.0)
    hn = jnp.where(i < n_i - 1, xn_ref[0] * s1 + sh, 0.0)
    ext = jnp.concatenate([hp, h, hn], axis=0)
    n_ext = ts + 2 * POOL_HALO
    pos = i * ts + lax.broadcasted_iota(jnp.int32, (ts, 1), 0)

    def shifted(a, k):
        return pltpu.roll(a, k % n_ext, axis=0)

    f = None
    for g, w in enumerate(POOL_WINDOWS):
        cs = slice(g * POOL_CH, (g + 1) * POOL_CH)
        e = ext[:, cs]
        acc = e + shifted(e, 1)
        half = 1
        while 2 * half < w:
            acc = shifted(acc, half) + shifted(acc, -half)
            half *= 2
        win = acc[POOL_HALO:POOL_HALO + ts]
        cnt = (jnp.minimum(pos + w // 2, seq) - jnp.maximum(pos - w // 2, 0)).astype(F32)
        pooled = win / cnt - h[:, cs]
        mixed = jnp.dot(pooled.astype(BF16), wg_ref[g], preferred_element_type=F32)
        mixed = mixed * ps_ref[:, cs]
        part = jnp.dot(mixed.astype(BF16), wo_ref[cs, :], preferred_element_type=F32)
        f = part if f is None else f + part
    y = alpha * x + g_ref[0] * f
    o_ref[0] = _layer_norm(y, lg_ref[...], lb_ref[...])


def _pool_layer(x, s1, sh1, g1, wg, ps, wo, lg, lb, alpha):
    b, seq, d = x.shape
    ts = TOKEN_TILE
    per = ts // POOL_HALO
    n_halo = seq // POOL_HALO
    vec = pl.BlockSpec((1, 1, d), lambda bi, i: (bi, 0, 0))
    row = pl.BlockSpec((1, d), lambda bi, i: (0, 0))
    return pl.pallas_call(
        functools.partial(_pool_kernel, seq=seq, ts=ts, alpha=alpha),
        grid=(b, seq // ts),
        in_specs=[
            pl.BlockSpec((1, POOL_HALO, d), lambda bi, i: (bi, jnp.maximum(i * per - 1, 0), 0)),
            pl.BlockSpec((1, ts, d), lambda bi, i: (bi, i, 0)),
            pl.BlockSpec((1, POOL_HALO, d),
                         lambda bi, i: (bi, jnp.minimum((i + 1) * per, n_halo - 1), 0)),
            vec, vec, vec,
            pl.BlockSpec(wg.shape, lambda bi, i: (0, 0, 0)),
            row,
            pl.BlockSpec((d, d), lambda bi, i: (0, 0)),
            row, row,
        ],
        out_specs=pl.BlockSpec((1, ts, d), lambda bi, i: (bi, i, 0)),
        out_shape=jax.ShapeDtypeStruct((b, seq, d), F32),
        compiler_params=_params(("arbitrary", "arbitrary")),
        name="pool_layer",
    )(x, x, x, s1, sh1, g1, wg, ps, wo, lg, lb)


def _qkv_kernel(x_ref, s_ref, sh_ref, w_ref, q_ref, k_ref, v_ref):
    h = x_ref[0] * (1.0 + s_ref[0]) + sh_ref[0]
    qkv = jnp.dot(h.astype(BF16), w_ref[...], preferred_element_type=F32)
    q_ref[0] = qkv[:, :Q_DIM].astype(BF16)
    k_ref[0] = qkv[:, Q_DIM:Q_DIM + KV_DIM].astype(BF16)
    v_ref[0] = qkv[:, Q_DIM + KV_DIM:].astype(BF16)


def _qkv(x, s1, sh1, w):
    b, seq, d = x.shape
    ts = TOKEN_TILE
    vec = pl.BlockSpec((1, 1, d), lambda bi, i: (bi, 0, 0))
    tok = lambda n: pl.BlockSpec((1, ts, n), lambda bi, i: (bi, i, 0))
    return pl.pallas_call(
        _qkv_kernel,
        grid=(b, seq // ts),
        in_specs=[tok(d), vec, vec, pl.BlockSpec(w.shape, lambda bi, i: (0, 0))],
        out_specs=[tok(Q_DIM), tok(KV_DIM), tok(KV_DIM)],
        out_shape=[jax.ShapeDtypeStruct((b, seq, Q_DIM), BF16),
                   jax.ShapeDtypeStruct((b, seq, KV_DIM), BF16),
                   jax.ShapeDtypeStruct((b, seq, KV_DIM), BF16)],
        compiler_params=_params(("arbitrary", "arbitrary")),
        name="attn_qkv",
    )(x, s1, sh1, w)


def _attn_kernel(sink_ref, q_ref, kp_ref, kc_ref, kn_ref, vp_ref, vc_ref, vn_ref, x_ref, g_ref,
                 wo_ref, lg_ref, lb_ref, o_ref, att_ref, *, seq, tq, alpha):
    i = pl.program_id(1)
    kcat = jnp.concatenate([kp_ref[0], kc_ref[0], kn_ref[0]], axis=0)
    vcat = jnp.concatenate([vp_ref[0], vc_ref[0], vn_ref[0]], axis=0)
    qi = lax.broadcasted_iota(jnp.int32, (BLOCK, 3 * BLOCK), 0)
    ki = lax.broadcasted_iota(jnp.int32, (BLOCK, 3 * BLOCK), 1)
    rel = jnp.abs(ki - BLOCK - qi)
    relf = rel.astype(F32)
    scale = HEAD_DIM ** -0.5
    slopes = _alibi_slopes()
    for j in range(tq // BLOCK):
        kpos = i * tq + (j - 1) * BLOCK + ki
        valid = (rel <= WINDOW) & (kpos >= 0) & (kpos < seq)
        rows = slice(j * BLOCK, (j + 1) * BLOCK)
        for hd in range(N_HEADS):
            hk = hd // GQA_GROUP
            qh = q_ref[0, rows, hd * HEAD_DIM:(hd + 1) * HEAD_DIM]
            kh = kcat[j * BLOCK:(j + 3) * BLOCK, hk * HEAD_DIM:(hk + 1) * HEAD_DIM]
            vh = vcat[j * BLOCK:(j + 3) * BLOCK, hk * HEAD_DIM:(hk + 1) * HEAD_DIM]
            s = lax.dot_general(qh, kh, (((1,), (1,)), ((), ())), preferred_element_type=F32)
            s = jnp.where(valid, s * scale - slopes[hd] * relf, NEG)
            sink = sink_ref[hd]
            m = jnp.maximum(jnp.max(s, axis=-1, keepdims=True), sink)
            p = jnp.exp(s - m)
            p = p / (jnp.sum(p, axis=-1, keepdims=True) + jnp.exp(sink - m))
            o = jnp.dot(p.astype(BF16), vh, preferred_element_type=F32)
            att_ref[rows, hd * HEAD_DIM:(hd + 1) * HEAD_DIM] = o.astype(BF16)
    f = jnp.dot(att_ref[...], wo_ref[...], preferred_element_type=F32)
    y = alpha * x_ref[0] + g_ref[0] * f
    o_ref[0] = _layer_norm(y, lg_ref[...], lb_ref[...])


def _attn_layer(x, q, k, v, g1, sink, wo, lg, lb, alpha):
    b, seq, d = x.shape
    tq = TOKEN_TILE
    per = tq // BLOCK
    nb = seq // BLOCK
    vec = pl.BlockSpec((1, 1, d), lambda bi, i: (bi, 0, 0))
    row = pl.BlockSpec((1, d), lambda bi, i: (0, 0))
    prev = pl.BlockSpec((1, BLOCK, KV_DIM), lambda bi, i: (bi, jnp.maximum(i * per - 1, 0), 0))
    cur = pl.BlockSpec((1, tq, KV_DIM), lambda bi, i: (bi, i, 0))
    nxt = pl.BlockSpec((1, BLOCK, KV_DIM),
                       lambda bi, i: (bi, jnp.minimum((i + 1) * per, nb - 1), 0))
    return pl.pallas_call(
        functools.partial(_attn_kernel, seq=seq, tq=tq, alpha=alpha),
        grid=(b, seq // tq),
        in_specs=[
            pl.BlockSpec(memory_space=pltpu.SMEM),
            pl.BlockSpec((1, tq, Q_DIM), lambda bi, i: (bi, i, 0)),
            prev, cur, nxt, prev, cur, nxt,
            pl.BlockSpec((1, tq, d), lambda bi, i: (bi, i, 0)),
            vec,
            pl.BlockSpec((d, d), lambda bi, i: (0, 0)),
            row, row,
        ],
        out_specs=pl.BlockSpec((1, tq, d), lambda bi, i: (bi, i, 0)),
        out_shape=jax.ShapeDtypeStruct((b, seq, d), F32),
        scratch_shapes=[pltpu.VMEM((tq, d), BF16)],
        compiler_params=_params(("arbitrary", "arbitrary")),
        name="attn_layer",
    )(sink, q, k, k, k, v, v, v, x, g1, wo, lg, lb)


def _top16_rows(scores, payloads):
    slot = lax.broadcasted_iota(jnp.int32, (PEER_TOPK, LANES), 0)
    rows = [lax.broadcasted_iota(jnp.int32, s.shape, 0).astype(F32) for s in scores]
    zero = jnp.zeros((PEER_TOPK, LANES), F32)

    def rnd(r, carry):
        out = []
        for (s, v, p), row, pay in zip(carry, rows, payloads):
            m = jnp.max(s, axis=0, keepdims=True)
            am = jnp.min(jnp.where(s == m, row, float(s.shape[0])), axis=0, keepdims=True)
            hit = row == am
            sel = am if pay is None else jnp.max(jnp.where(hit, pay, -1.0), axis=0, keepdims=True)
            out.append((jnp.where(hit, -jnp.inf, s), jnp.where(slot == r, m, v),
                        jnp.where(slot == r, sel, p)))
        return tuple(out)

    done = lax.fori_loop(0, PEER_TOPK, rnd, tuple((s, zero, zero) for s in scores))
    return [(v, p) for _, v, p in done]


def _route_kernel(x_ref, s_ref, sh_ref, wqt_ref, keys_ref, eidx_ref, gate_ref,
                  qt_ref, ei_ref, g_ref, *, tr):
    n_lt = tr // LANES
    h = x_ref[0] * (1.0 + s_ref[0]) + sh_ref[0]
    qt = lax.dot_general(wqt_ref[...], h.astype(BF16), (((1,), (1,)), ((), ())),
                         preferred_element_type=F32)
    for lt in range(n_lt):
        for hc in range(2 * PEER_HEADS):
            qt_ref[lt, hc] = qt[hc * N_KEYS:(hc + 1) * N_KEYS,
                                lt * LANES:(lt + 1) * LANES].astype(BF16)
    row8 = lax.broadcasted_iota(jnp.int32, (SUBLANES, LANES), 0)

    def job(hh, lt):
        sc = [jnp.dot(keys_ref[hh, c], qt_ref[lt, hh * 2 + c], preferred_element_type=F32)
              for c in range(2)]
        (v1, i1), (v2, i2) = _top16_rows(sc, [None, None])
        cand, cidx = [], []
        for i in range(PEER_TOPK):
            n_j = PEER_TOPK // (i + 1)
            a = v1[i:i + 1, :]
            ai = i1[i:i + 1, :] * float(N_KEYS)
            for j0 in range(0, n_j, SUBLANES):
                ok = row8 < (n_j - j0)
                cand.append(jnp.where(ok, a + v2[j0:j0 + SUBLANES, :], -jnp.inf))
                cidx.append(ai + i2[j0:j0 + SUBLANES, :])
        cand = jnp.concatenate(cand, axis=0)
        cidx = jnp.concatenate(cidx, axis=0)
        ((best, eid),) = _top16_rows([cand], [cidx])
        ex = jnp.exp(best - best[0:1, :])
        gate = ex / jnp.sum(ex, axis=0, keepdims=True)
        dst = pl.ds(pl.multiple_of(hh * PEER_TOPK, PEER_TOPK), PEER_TOPK)
        ei_ref[lt, dst, :] = eid.astype(jnp.int32)
        g_ref[lt, dst, :] = gate

    def head(hh, carry):
        for lt in range(n_lt):
            job(hh, lt)
        return carry

    lax.fori_loop(0, PEER_HEADS, head, 0)
    for lt in range(n_lt):
        eidx_ref[0, :, lt * LANES:(lt + 1) * LANES] = ei_ref[lt]
        gate_ref[0, :, lt * LANES:(lt + 1) * LANES] = g_ref[lt]


def _route(x, s2, sh2, wqt, keys):
    b, seq, d = x.shape
    tr = ROUTE_TILE
    n_lt = tr // LANES
    vec = pl.BlockSpec((1, 1, d), lambda bi, i: (bi, 0, 0))
    out = pl.BlockSpec((1, PEER_SEL, tr), lambda bi, i: (bi, 0, i))
    return pl.pallas_call(
        functools.partial(_route_kernel, tr=tr),
        grid=(b, seq // tr),
        in_specs=[
            pl.BlockSpec((1, tr, d), lambda bi, i: (bi, i, 0)),
            vec, vec,
            pl.BlockSpec(wqt.shape, lambda bi, i: (0, 0)),
            pl.BlockSpec(keys.shape, lambda bi, i: (0, 0, 0, 0)),
        ],
        out_specs=[out, out],
        out_shape=[jax.ShapeDtypeStruct((b, PEER_SEL, seq), jnp.int32),
                   jax.ShapeDtypeStruct((b, PEER_SEL, seq), F32)],
        scratch_shapes=[pltpu.VMEM((n_lt, 2 * PEER_HEADS, N_KEYS, LANES), BF16),
                        pltpu.VMEM((n_lt, PEER_SEL, LANES), jnp.int32),
                        pltpu.VMEM((n_lt, PEER_SEL, LANES), F32)],
        compiler_params=_params(("arbitrary", "arbitrary")),
        name="peer_route",
    )(x, s2, sh2, wqt, keys)


def _row_sums8(ps, sub):
    def merge(a, b, d):
        va = a + pltpu.roll(a, SUBLANES - d, axis=0)
        vb = b + pltpu.roll(b, d, axis=0)
        return jnp.where((sub & d) == 0, va, vb)
    y = [merge(ps[i], ps[i + 4], 4) for i in range(4)]
    x0 = merge(y[0], y[2], 2)
    x1 = merge(y[1], y[3], 2)
    return merge(x0, x1, 1)


def _expert_kernel(idx_ref, gate_ref, x_ref, s_ref, sh_ref, uv_ref, o_ref, *scratch, tb, nbuf):
    bufs = scratch[:nbuf]
    sem = scratch[nbuf]
    wb_ref = scratch[nbuf + 1]
    g_ref = scratch[nbuf + 2]
    q_ref = scratch[nbuf + 3]
    ahead = nbuf - 2
    s2 = 1.0 + s_ref[0]
    sh = sh_ref[0]
    sub = lax.broadcasted_iota(jnp.int32, (SUBLANES, LANES), 0)
    g_ref[...] = gate_ref[0].T

    n_chunks = PEER_SEL // SUBLANES

    def issue_chunk(t, slot, c):
        for e in range(c * SUBLANES, (c + 1) * SUBLANES):
            pltpu.make_async_copy(uv_ref.at[idx_ref[e, t]], bufs[slot].at[e],
                                  sem.at[slot]).start(priority=e % 2)

    def wait(slot):
        pltpu.make_async_copy(uv_ref.at[pl.ds(0, PEER_SEL)], bufs[slot], sem.at[slot]).wait()

    def dots_chunk(xt, slot, c):
        ps = []
        for k in range(SUBLANES):
            p = bufs[slot][c * SUBLANES + k, 0:ROW_TILES, :].astype(F32) * xt
            ps.append(p[0:SUBLANES] + p[SUBLANES:2 * SUBLANES])
        return _row_sums8(ps, sub)

    def weights(t, par):
        a = jnp.sum(q_ref[par].T, axis=0, keepdims=True)
        act = 0.5 * a * (1.0 + lax.erf(a * np.float32(np.sqrt(0.5))))
        w = act * g_ref[pl.ds(t, 1), :]
        wb_ref[par] = jnp.broadcast_to(w, (LANES, PEER_SEL)).T

    def combine_chunk(accs, slot, c):
        for e in range(c * SUBLANES, (c + 1) * SUBLANES):
            term = (wb_ref[slot % 2, e:e + 1, :]
                    * bufs[slot][e, ROW_TILES:2 * ROW_TILES, :].astype(F32))
            accs[e % 4] = term if accs[e % 4] is None else accs[e % 4] + term

    def step(t, k, do_issue, do_dots, do_combine):
        slot_i, slot_d = (k + ahead) % nbuf, (k + 2) % nbuf
        if do_dots:
            wait(slot_d)
            xt = x_ref[t + 2] * s2 + sh
        if do_combine:
            weights(t, k % 2)
        for c in range(n_chunks):
            if do_issue:
                issue_chunk(t + ahead, slot_i, c)
            if do_dots:
                q_ref[k % 2, c * SUBLANES:(c + 1) * SUBLANES, :] = dots_chunk(xt, slot_d, c)
        if do_combine:
            accs = [None] * 4
            for c in range(n_chunks):
                combine_chunk(accs, k, c)
            o_ref[t] = (accs[0] + accs[1]) + (accs[2] + accs[3])

    for t in range(ahead - 2):
        for c in range(n_chunks):
            issue_chunk(t, t, c)
    step(-2, nbuf - 2, True, True, False)
    step(-1, nbuf - 1, True, True, False)

    def group(gi, carry):
        for k in range(nbuf):
            step(gi * nbuf + k, k, True, True, True)
        return carry

    n_groups = tb // nbuf
    lax.fori_loop(0, n_groups - 1, group, 0)
    for k in range(nbuf):
        t = (n_groups - 1) * nbuf + k
        step(t, k, t + ahead < tb, t + 2 < tb, True)


def _experts(eidx, gate_t, x3, s2, sh2, uv, seq):
    t_all = x3.shape[0]
    tb, nbuf = PEER_TB, PEER_NBUF
    per_seq = seq // tb
    tok = pl.BlockSpec((tb, ROW_TILES, LANES), lambda i: (i, 0, 0))
    vec = pl.BlockSpec((1, ROW_TILES, LANES), lambda i: (i // per_seq, 0, 0))
    return pl.pallas_call(
        functools.partial(_expert_kernel, tb=tb, nbuf=nbuf),
        grid=(t_all // tb,),
        in_specs=[
            pl.BlockSpec((None, PEER_SEL, tb), lambda i: (i // per_seq, 0, i % per_seq),
                         memory_space=pltpu.SMEM),
            pl.BlockSpec((1, PEER_SEL, tb), lambda i: (i // per_seq, 0, i % per_seq)),
            tok, vec, vec,
            pl.BlockSpec(memory_space=pl.ANY),
        ],
        out_specs=tok,
        out_shape=jax.ShapeDtypeStruct(x3.shape, F32),
        scratch_shapes=(
            [pltpu.VMEM((PEER_SEL, 2 * ROW_TILES, LANES), uv.dtype) for _ in range(nbuf)]
            + [pltpu.SemaphoreType.DMA((nbuf,)),
               pltpu.VMEM((2, PEER_SEL, LANES), F32),
               pltpu.VMEM((tb, PEER_SEL), F32),
               pltpu.VMEM((2, PEER_SEL, LANES), F32)]),
        compiler_params=_params(("arbitrary",)),
        name="peer_experts",
    )(eidx, gate_t, x3, s2, sh2, uv)


def _res_ln_kernel(x_ref, f_ref, g_ref, lg_ref, lb_ref, o_ref, *, alpha):
    y = alpha * x_ref[0] + g_ref[0] * f_ref[0]
    o_ref[0] = _layer_norm(y, lg_ref[...], lb_ref[...])


def _res_ln(x, f, g, lg, lb, alpha):
    b, seq, d = x.shape
    ts = TOKEN_TILE
    tok = pl.BlockSpec((1, ts, d), lambda bi, i: (bi, i, 0))
    row = pl.BlockSpec((1, d), lambda bi, i: (0, 0))
    return pl.pallas_call(
        functools.partial(_res_ln_kernel, alpha=alpha),
        grid=(b, seq // ts),
        in_specs=[tok, tok, pl.BlockSpec((1, 1, d), lambda bi, i: (bi, 0, 0)), row, row],
        out_specs=tok,
        out_shape=jax.ShapeDtypeStruct(x.shape, F32),
        compiler_params=_params(("arbitrary", "arbitrary")),
        name="res_ln",
    )(x, f, g, lg, lb)


def _trunk(x, mod, w):
    b, seq, d = x.shape
    depth = mod.shape[0]
    alpha = float((2 * depth) ** 0.25)
    for i in range(depth):
        sh1, s1, g1, sh2, s2, g2 = [m.reshape(b, 1, d) for m in jnp.split(mod[i], 6, axis=-1)]
        lw = w[i]
        if i % 2 == 0:
            x = _pool_layer(x, s1, sh1, g1, lw["pool_w"], lw["pool_scale"], lw["wo"],
                            lw["ln1_g"], lw["ln1_b"], alpha)
        else:
            q, k, v = _qkv(x, s1, sh1, lw["wqkv"])
            x = _attn_layer(x, q, k, v, g1, lw["sink"], lw["wo"], lw["ln1_g"], lw["ln1_b"], alpha)
        eidx, gate_t = _route(x, s2, sh2, lw["peer_wqt"], lw["peer_keys"])
        f3 = _experts(eidx, gate_t,
                      x.reshape(b * seq, ROW_TILES, LANES),
                      s2.reshape(b, ROW_TILES, LANES), sh2.reshape(b, ROW_TILES, LANES),
                      lw["uv"], seq)
        x = _res_ln(x, f3.reshape(b, seq, d), g2, lw["ln2_g"], lw["ln2_b"], alpha)
    return x


def kernel(x_prompt, x_sample, c_prompt, c_sample, ada_w, ada_b, ln1_g, ln1_b, ln2_g, ln2_b,
           pool_w, pool_scale, pool_wo, attn_wqkv, attn_sink, attn_wo,
           peer_wq, peer_keys, peer_u, peer_v):
    depth, d = ln1_g.shape
    n_exp = peer_u.shape[1]
    bp, bs = c_prompt.shape[0], c_sample.shape[0]
    m_pad = -(bp + bs) % SUBLANES
    c_all = jnp.concatenate([c_prompt, c_sample, jnp.zeros((m_pad, d), F32)], axis=0)
    mod = _ada(c_all, ada_w, ada_b)

    layers = []
    for i in range(depth):
        j = i // 2
        lw = {
            "ln1_g": ln1_g[i].reshape(1, d), "ln1_b": ln1_b[i].reshape(1, d),
            "ln2_g": ln2_g[i].reshape(1, d), "ln2_b": ln2_b[i].reshape(1, d),
            "peer_wqt": peer_wq[i].T.astype(BF16),
            "peer_keys": peer_keys[i].astype(BF16),
            "uv": jnp.concatenate([peer_u[i].astype(BF16), peer_v[i].astype(BF16)],
                                  axis=-1).reshape(n_exp, 2 * ROW_TILES, LANES),
        }
        if i % 2 == 0:
            lw.update(pool_w=pool_w[j].astype(BF16), pool_scale=pool_scale[j].reshape(1, d),
                      wo=pool_wo[j].astype(BF16))
        else:
            lw.update(wqkv=attn_wqkv[j].astype(BF16), sink=attn_sink[j],
                      wo=attn_wo[j].astype(BF16))
        layers.append(lw)

    y_prompt = _trunk(x_prompt, mod[:, :bp], layers)
    y_sample = _trunk(x_sample, mod[:, bp:bp + bs], layers)
    return (y_prompt, y_sample)
```

```python
import functools

import numpy as np
import jax
import jax.numpy as jnp
from jax import lax
from jax.experimental import pallas as pl
from jax.experimental.pallas import tpu as pltpu

F32 = jnp.float32
BF16 = jnp.bfloat16

D_MODEL = 2048
POOL_WINDOWS = (2, 4, 8, 16)
POOL_CH = D_MODEL // len(POOL_WINDOWS)
POOL_HALO = 8
HEAD_DIM = 128
N_HEADS = D_MODEL // HEAD_DIM
N_KV_HEADS = 4
GQA_GROUP = N_HEADS // N_KV_HEADS
WINDOW = 128
BLOCK = 128
Q_DIM = N_HEADS * HEAD_DIM
KV_DIM = N_KV_HEADS * HEAD_DIM
N_KEYS = 128
PEER_HEADS = 8
PEER_TOPK = 16
PEER_SEL = PEER_HEADS * PEER_TOPK
LN_EPS = 1e-5
NEG = -1e30

LANES = 128
SUBLANES = 8
ROW_TILES = D_MODEL // LANES
VMEM_LIMIT = 52 * 1024 * 1024

ADA_TN = 1024
TOKEN_TILE = 256
ROUTE_TILE = 256
PEER_TB = 128
PEER_NBUF = 8


def _alibi_slopes():
    return [2.0 ** (-8.0 * (h + 1) / N_HEADS) for h in range(N_HEADS)]


def _params(sem):
    return pltpu.CompilerParams(dimension_semantics=sem, vmem_limit_bytes=VMEM_LIMIT)


def _layer_norm(y, g, b):
    mu = jnp.mean(y, axis=-1, keepdims=True)
    d = y - mu
    var = jnp.mean(d * d, axis=-1, keepdims=True)
    return d * lax.rsqrt(var + LN_EPS) * g + b


def _ada_kernel(c_ref, w_ref, b_ref, o_ref):
    c = c_ref[...]
    sc = c * jax.nn.sigmoid(c)
    o_ref[0] = jnp.dot(sc.astype(BF16), w_ref[0].astype(BF16),
                       preferred_element_type=F32) + b_ref[0]


def _ada(c_all, ada_w, ada_b):
    depth, d, n = ada_w.shape
    m = c_all.shape[0]
    return pl.pallas_call(
        _ada_kernel,
        grid=(depth, n // ADA_TN),
        in_specs=[
            pl.BlockSpec((m, d), lambda l, j: (0, 0)),
            pl.BlockSpec((1, d, ADA_TN), lambda l, j: (l, 0, j)),
            pl.BlockSpec((1, 1, ADA_TN), lambda l, j: (l, 0, j)),
        ],
        out_specs=pl.BlockSpec((1, m, ADA_TN), lambda l, j: (l, 0, j)),
        out_shape=jax.ShapeDtypeStruct((depth, m, n), F32),
        compiler_params=_params(("arbitrary", "arbitrary")),
        name="ada_mod",
    )(c_all, ada_w, ada_b.reshape(depth, 1, n))


def _pool_kernel(xp_ref, x_ref, xn_ref, s_ref, sh_ref, g_ref, wg_ref, ps_ref, wo_ref,
                 lg_ref, lb_ref, o_ref, *, seq, ts, alpha):
    i = pl.program_id(1)
    n_i = pl.num_programs(1)
    s1 = 1.0 + s_ref[0]
    sh = sh_ref[0]
    x = x_ref[0]
    h = x * s1 + sh
    hp = jnp.where(i > 0, xp_ref[0] * s1 + sh, 0.0)
    hn = jnp.where(i < n_i - 1, xn_ref[0] * s1 + sh, 0.0)
    ext = jnp.concatenate([hp, h, hn], axis=0)
    n_ext = ts + 2 * POOL_HALO
    pos = i * ts + lax.broadcasted_iota(jnp.int32, (ts, 1), 0)

    def shifted(a, k):
        return pltpu.roll(a, k % n_ext, axis=0)

    f = None
    for g, w in enumerate(POOL_WINDOWS):
        cs = slice(g * POOL_CH, (g + 1) * POOL_CH)
        e = ext[:, cs]
        acc = e + shifted(e, 1)
        half = 1
        while 2 * half < w:
            acc = shifted(acc, half) + shifted(acc, -half)
            half *= 2
        win = acc[POOL_HALO:POOL_HALO + ts]
        cnt = (jnp.minimum(pos + w // 2, seq) - jnp.maximum(pos - w // 2, 0)).astype(F32)
        pooled = win / cnt - h[:, cs]
        mixed = jnp.dot(pooled.astype(BF16), wg_ref[g], preferred_element_type=F32)
        mixed = mixed * ps_ref[:, cs]
        part = jnp.dot(mixed.astype(BF16), wo_ref[cs, :], preferred_element_type=F32)
        f = part if f is None else f + part
    y = alpha * x + g_ref[0] * f
    o_ref[0] = _layer_norm(y, lg_ref[...], lb_ref[...])


def _pool_layer(x, s1, sh1, g1, wg, ps, wo, lg, lb, alpha):
    b, seq, d = x.shape
    ts = TOKEN_TILE
    per = ts // POOL_HALO
    n_halo = seq // POOL_HALO
    vec = pl.BlockSpec((1, 1, d), lambda bi, i: (bi, 0, 0))
    row = pl.BlockSpec((1, d), lambda bi, i: (0, 0))
    return pl.pallas_call(
        functools.partial(_pool_kernel, seq=seq, ts=ts, alpha=alpha),
        grid=(b, seq // ts),
        in_specs=[
            pl.BlockSpec((1, POOL_HALO, d), lambda bi, i: (bi, jnp.maximum(i * per - 1, 0), 0)),
            pl.BlockSpec((1, ts, d), lambda bi, i: (bi, i, 0)),
            pl.BlockSpec((1, POOL_HALO, d),
                         lambda bi, i: (bi, jnp.minimum((i + 1) * per, n_halo - 1), 0)),
            vec, vec, vec,
            pl.BlockSpec(wg.shape, lambda bi, i: (0, 0, 0)),
            row,
            pl.BlockSpec((d, d), lambda bi, i: (0, 0)),
            row, row,
        ],
        out_specs=pl.BlockSpec((1, ts, d), lambda bi, i: (bi, i, 0)),
        out_shape=jax.ShapeDtypeStruct((b, seq, d), F32),
        compiler_params=_params(("arbitrary", "arbitrary")),
        name="pool_layer",
    )(x, x, x, s1, sh1, g1, wg, ps, wo, lg, lb)


def _qkv_kernel(x_ref, s_ref, sh_ref, w_ref, q_ref, k_ref, v_ref):
    h = x_ref[0] * (1.0 + s_ref[0]) + sh_ref[0]
    qkv = jnp.dot(h.astype(BF16), w_ref[...], preferred_element_type=F32)
    q_ref[0] = qkv[:, :Q_DIM].astype(BF16)
    k_ref[0] = qkv[:, Q_DIM:Q_DIM + KV_DIM].astype(BF16)
    v_ref[0] = qkv[:, Q_DIM + KV_DIM:].astype(BF16)


def _qkv(x, s1, sh1, w):
    b, seq, d = x.shape
    ts = TOKEN_TILE
    vec = pl.BlockSpec((1, 1, d), lambda bi, i: (bi, 0, 0))
    tok = lambda n: pl.BlockSpec((1, ts, n), lambda bi, i: (bi, i, 0))
    return pl.pallas_call(
        _qkv_kernel,
        grid=(b, seq // ts),
        in_specs=[tok(d), vec, vec, pl.BlockSpec(w.shape, lambda bi, i: (0, 0))],
        out_specs=[tok(Q_DIM), tok(KV_DIM), tok(KV_DIM)],
        out_shape=[jax.ShapeDtypeStruct((b, seq, Q_DIM), BF16),
                   jax.ShapeDtypeStruct((b, seq, KV_DIM), BF16),
                   jax.ShapeDtypeStruct((b, seq, KV_DIM), BF16)],
        compiler_params=_params(("arbitrary", "arbitrary")),
        name="attn_qkv",
    )(x, s1, sh1, w)


def _attn_kernel(sink_ref, q_ref, kp_ref, kc_ref, kn_ref, vp_ref, vc_ref, vn_ref, x_ref, g_ref,
                 wo_ref, lg_ref, lb_ref, o_ref, att_ref, *, seq, tq, alpha):
    i = pl.program_id(1)
    kcat = jnp.concatenate([kp_ref[0], kc_ref[0], kn_ref[0]], axis=0)
    vcat = jnp.concatenate([vp_ref[0], vc_ref[0], vn_ref[0]], axis=0)
    qi = lax.broadcasted_iota(jnp.int32, (BLOCK, 3 * BLOCK), 0)
    ki = lax.broadcasted_iota(jnp.int32, (BLOCK, 3 * BLOCK), 1)
    rel = jnp.abs(ki - BLOCK - qi)
    relf = rel.astype(F32)
    scale = HEAD_DIM ** -0.5
    slopes = _alibi_slopes()
    for j in range(tq // BLOCK):
        kpos = i * tq + (j - 1) * BLOCK + ki
        valid = (rel <= WINDOW) & (kpos >= 0) & (kpos < seq)
        rows = slice(j * BLOCK, (j + 1) * BLOCK)
        for hd in range(N_HEADS):
            hk = hd // GQA_GROUP
            qh = q_ref[0, rows, hd * HEAD_DIM:(hd + 1) * HEAD_DIM]
            kh = kcat[j * BLOCK:(j + 3) * BLOCK, hk * HEAD_DIM:(hk + 1) * HEAD_DIM]
            vh = vcat[j * BLOCK:(j + 3) * BLOCK, hk * HEAD_DIM:(hk + 1) * HEAD_DIM]
            s = lax.dot_general(qh, kh, (((1,), (1,)), ((), ())), preferred_element_type=F32)
            s = jnp.where(valid, s * scale - slopes[hd] * relf, NEG)
            sink = sink_ref[hd]
            m = jnp.maximum(jnp.max(s, axis=-1, keepdims=True), sink)
            p = jnp.exp(s - m)
            p = p / (jnp.sum(p, axis=-1, keepdims=True) + jnp.exp(sink - m))
            o = jnp.dot(p.astype(BF16), vh, preferred_element_type=F32)
            att_ref[rows, hd * HEAD_DIM:(hd + 1) * HEAD_DIM] = o.astype(BF16)
    f = jnp.dot(att_ref[...], wo_ref[...], preferred_element_type=F32)
    y = alpha * x_ref[0] + g_ref[0] * f
    o_ref[0] = _layer_norm(y, lg_ref[...], lb_ref[...])


def _attn_layer(x, q, k, v, g1, sink, wo, lg, lb, alpha):
    b, seq, d = x.shape
    tq = TOKEN_TILE
    per = tq // BLOCK
    nb = seq // BLOCK
    vec = pl.BlockSpec((1, 1, d), lambda bi, i: (bi, 0, 0))
    row = pl.BlockSpec((1, d), lambda bi, i: (0, 0))
    prev = pl.BlockSpec((1, BLOCK, KV_DIM), lambda bi, i: (bi, jnp.maximum(i * per - 1, 0), 0))
    cur = pl.BlockSpec((1, tq, KV_DIM), lambda bi, i: (bi, i, 0))
    nxt = pl.BlockSpec((1, BLOCK, KV_DIM),
                       lambda bi, i: (bi, jnp.minimum((i + 1) * per, nb - 1), 0))
    return pl.pallas_call(
        functools.partial(_attn_kernel, seq=seq, tq=tq, alpha=alpha),
        grid=(b, seq // tq),
        in_specs=[
            pl.BlockSpec(memory_space=pltpu.SMEM),
            pl.BlockSpec((1, tq, Q_DIM), lambda bi, i: (bi, i, 0)),
            prev, cur, nxt, prev, cur, nxt,
            pl.BlockSpec((1, tq, d), lambda bi, i: (bi, i, 0)),
            vec,
            pl.BlockSpec((d, d), lambda bi, i: (0, 0)),
            row, row,
        ],
        out_specs=pl.BlockSpec((1, tq, d), lambda bi, i: (bi, i, 0)),
        out_shape=jax.ShapeDtypeStruct((b, seq, d), F32),
        scratch_shapes=[pltpu.VMEM((tq, d), BF16)],
        compiler_params=_params(("arbitrary", "arbitrary")),
        name="attn_layer",
    )(sink, q, k, k, k, v, v, v, x, g1, wo, lg, lb)


def _top16_rows(scores, payloads):
    slot = lax.broadcasted_iota(jnp.int32, (PEER_TOPK, LANES), 0)
    rows = [lax.broadcasted_iota(jnp.int32, s.shape, 0).astype(F32) for s in scores]
    zero = jnp.zeros((PEER_TOPK, LANES), F32)

    def rnd(r, carry):
        out = []
        for (s, v, p), row, pay in zip(carry, rows, payloads):
            m = jnp.max(s, axis=0, keepdims=True)
            am = jnp.min(jnp.where(s == m, row, float(s.shape[0])), axis=0, keepdims=True)
            hit = row == am
            sel = am if pay is None else jnp.max(jnp.where(hit, pay, -1.0), axis=0, keepdims=True)
            out.append((jnp.where(hit, -jnp.inf, s), jnp.where(slot == r, m, v),
                        jnp.where(slot == r, sel, p)))
        return tuple(out)

    done = lax.fori_loop(0, PEER_TOPK, rnd, tuple((s, zero, zero) for s in scores))
    return [(v, p) for _, v, p in done]


CAND_ROWS = 64


def _candidate_tables():
    pairs = [(i, j) for i in range(PEER_TOPK) for j in range(PEER_TOPK)
             if (i + 1) * (j + 1) <= PEER_TOPK]
    e1 = np.zeros((CAND_ROWS, LANES), np.float32)
    e2 = np.zeros((CAND_ROWS, LANES), np.float32)
    for r, (i, j) in enumerate(pairs):
        e1[r, i] = 1.0
        e2[r, j] = 1.0
    return e1, e2, len(pairs)


def _route_kernel(x_ref, s_ref, sh_ref, wqt_ref, keys_ref, e1_ref, e2_ref, eidx_ref, gate_ref,
                  qt_ref, ei_ref, g_ref, *, tr, n_cand):
    n_lt = tr // LANES
    h = x_ref[0] * (1.0 + s_ref[0]) + sh_ref[0]
    qt = lax.dot_general(wqt_ref[...], h.astype(BF16), (((1,), (1,)), ((), ())),
                         preferred_element_type=F32)
    for lt in range(n_lt):
        for hc in range(2 * PEER_HEADS):
            qt_ref[lt, hc] = qt[hc * N_KEYS:(hc + 1) * N_KEYS,
                                lt * LANES:(lt + 1) * LANES].astype(BF16)
    cand_row = lax.broadcasted_iota(jnp.int32, (CAND_ROWS, LANES), 0)
    pad = jnp.zeros((LANES - PEER_TOPK, LANES), F32)
    hi = lax.Precision.HIGHEST

    def pick(sel_ref, top, exact_in_bf16=False):
        full = jnp.concatenate([top, pad], axis=0)
        if exact_in_bf16:
            return jnp.dot(sel_ref[...].astype(BF16), full.astype(BF16),
                           preferred_element_type=F32)
        return jnp.dot(sel_ref[...], full, precision=hi, preferred_element_type=F32)

    def job(hh, lt):
        sc = [jnp.dot(keys_ref[hh, c], qt_ref[lt, hh * 2 + c], preferred_element_type=F32)
              for c in range(2)]
        (v1, i1), (v2, i2) = _top16_rows(sc, [None, None])
        cand = jnp.where(cand_row < n_cand, pick(e1_ref, v1) + pick(e2_ref, v2), -jnp.inf)
        cidx = pick(e1_ref, i1, True) * float(N_KEYS) + pick(e2_ref, i2, True)
        ((best, eid),) = _top16_rows([cand], [cidx])
        ex = jnp.exp(best - best[0:1, :])
        gate = ex / jnp.sum(ex, axis=0, keepdims=True)
        dst = pl.ds(pl.multiple_of(hh * PEER_TOPK, PEER_TOPK), PEER_TOPK)
        ei_ref[lt, dst, :] = eid.astype(jnp.int32)
        g_ref[lt, dst, :] = gate

    def head(hh, carry):
        for lt in range(n_lt):
            job(hh, lt)
        return carry

    lax.fori_loop(0, PEER_HEADS, head, 0)
    for lt in range(n_lt):
        eidx_ref[0, lt * LANES:(lt + 1) * LANES, :] = ei_ref[lt].T
        gate_ref[0, lt * LANES:(lt + 1) * LANES, :] = g_ref[lt].T


def _route(x, s2, sh2, wqt, keys):
    b, seq, d = x.shape
    tr = ROUTE_TILE
    n_lt = tr // LANES
    e1, e2, n_cand = _candidate_tables()
    vec = pl.BlockSpec((1, 1, d), lambda bi, i: (bi, 0, 0))
    sel = pl.BlockSpec((CAND_ROWS, LANES), lambda bi, i: (0, 0))
    out = pl.BlockSpec((1, tr, PEER_SEL), lambda bi, i: (bi, i, 0))
    return pl.pallas_call(
        functools.partial(_route_kernel, tr=tr, n_cand=n_cand),
        grid=(b, seq // tr),
        in_specs=[
            pl.BlockSpec((1, tr, d), lambda bi, i: (bi, i, 0)),
            vec, vec,
            pl.BlockSpec(wqt.shape, lambda bi, i: (0, 0)),
            pl.BlockSpec(keys.shape, lambda bi, i: (0, 0, 0, 0)),
            sel, sel,
        ],
        out_specs=[out, out],
        out_shape=[jax.ShapeDtypeStruct((b, seq, PEER_SEL), jnp.int32),
                   jax.ShapeDtypeStruct((b, seq, PEER_SEL), F32)],
        scratch_shapes=[pltpu.VMEM((n_lt, 2 * PEER_HEADS, N_KEYS, LANES), BF16),
                        pltpu.VMEM((n_lt, PEER_SEL, LANES), jnp.int32),
                        pltpu.VMEM((n_lt, PEER_SEL, LANES), F32)],
        compiler_params=_params(("arbitrary", "arbitrary")),
        name="peer_route",
    )(x, s2, sh2, wqt, keys, jnp.asarray(e1), jnp.asarray(e2))


def _row_sums8(ps, sub):
    def merge(a, b, d):
        va = a + pltpu.roll(a, SUBLANES - d, axis=0)
        vb = b + pltpu.roll(b, d, axis=0)
        return jnp.where((sub & d) == 0, va, vb)
    y = [merge(ps[i], ps[i + 4], 4) for i in range(4)]
    x0 = merge(y[0], y[2], 2)
    x1 = merge(y[1], y[3], 2)
    return merge(x0, x1, 1)


def _expert_kernel(idx_ref, gate_ref, x_ref, s_ref, sh_ref, uv_ref, o_ref, *scratch, tb, nbuf):
    bufs = scratch[:nbuf]
    sem = scratch[nbuf]
    wb_ref = scratch[nbuf + 1]
    q_ref = scratch[nbuf + 2]
    w_ref = scratch[nbuf + 3]
    ahead = nbuf - 2
    s2 = 1.0 + s_ref[0]
    sh = sh_ref[0]
    sub = lax.broadcasted_iota(jnp.int32, (SUBLANES, LANES), 0)

    n_chunks = PEER_SEL // SUBLANES

    def issue_chunk(t, slot, c):
        for e in range(c * SUBLANES, (c + 1) * SUBLANES):
            pltpu.make_async_copy(uv_ref.at[idx_ref[t, e]], bufs[slot].at[e],
                                  sem.at[slot]).start(priority=e % 2)

    def wait(slot):
        pltpu.make_async_copy(uv_ref.at[pl.ds(0, PEER_SEL)], bufs[slot], sem.at[slot]).wait()

    def dots_chunk(xt, slot, c):
        ps = []
        for k in range(SUBLANES):
            p = bufs[slot][c * SUBLANES + k, 0:ROW_TILES, :].astype(F32) * xt
            ps.append(p[0:SUBLANES] + p[SUBLANES:2 * SUBLANES])
        return _row_sums8(ps, sub)

    def gated(t, par):
        a = jnp.sum(q_ref[par].T, axis=0, keepdims=True)
        act = 0.5 * a * (1.0 + lax.erf(a * np.float32(np.sqrt(0.5))))
        w_ref[par] = jnp.broadcast_to(act * gate_ref[pl.ds(t, 1), :], (SUBLANES, PEER_SEL))

    def spread(par):
        w = w_ref[par, 0:1, :]
        wb_ref[par] = jnp.broadcast_to(w, (LANES, PEER_SEL)).T

    def combine_chunk(accs, slot, c):
        for e in range(c * SUBLANES, (c + 1) * SUBLANES):
            term = (wb_ref[slot % 2, e:e + 1, :]
                    * bufs[slot][e, ROW_TILES:2 * ROW_TILES, :].astype(F32))
            accs[e % 4] = term if accs[e % 4] is None else accs[e % 4] + term

    def step(t, k, t_static=None):
        live = lambda d: t_static is None or 0 <= t_static + d < tb
        slot_i, slot_d = (k + ahead) % nbuf, (k + 3) % nbuf
        if live(3):
            wait(slot_d)
            xt = x_ref[t + 3] * s2 + sh
        if live(2):
            gated(t + 2, k % 2)
        if live(1):
            spread((k + 1) % 2)
        accs = [None] * 4
        for c in range(n_chunks):
            if live(ahead):
                issue_chunk(t + ahead, slot_i, c)
            if live(3):
                q_ref[(k + 1) % 2, c * SUBLANES:(c + 1) * SUBLANES, :] = dots_chunk(xt, slot_d, c)
            if live(0):
                combine_chunk(accs, k, c)
        if live(0):
            o_ref[t] = (accs[0] + accs[1]) + (accs[2] + accs[3])

    for t in range(ahead - 3):
        for c in range(n_chunks):
            issue_chunk(t, t, c)
    for t in range(-3, 0):
        step(t, t % nbuf, t)

    def group(gi, carry):
        for k in range(nbuf):
            step(gi * nbuf + k, k)
        return carry

    n_groups = tb // nbuf
    lax.fori_loop(0, n_groups - 1, group, 0)
    for k in range(nbuf):
        t = (n_groups - 1) * nbuf + k
        step(t, k, t)


def _experts(eidx, gate, x3, s2, sh2, uv, seq):
    t_all = x3.shape[0]
    tb, nbuf = PEER_TB, PEER_NBUF
    per_seq = seq // tb
    tok = pl.BlockSpec((tb, ROW_TILES, LANES), lambda i: (i, 0, 0))
    vec = pl.BlockSpec((1, ROW_TILES, LANES), lambda i: (i // per_seq, 0, 0))
    return pl.pallas_call(
        functools.partial(_expert_kernel, tb=tb, nbuf=nbuf),
        grid=(t_all // tb,),
        in_specs=[
            pl.BlockSpec((tb, PEER_SEL), lambda i: (i, 0), memory_space=pltpu.SMEM),
            pl.BlockSpec((tb, PEER_SEL), lambda i: (i, 0)),
            tok, vec, vec,
            pl.BlockSpec(memory_space=pl.ANY),
        ],
        out_specs=tok,
        out_shape=jax.ShapeDtypeStruct(x3.shape, F32),
        scratch_shapes=(
            [pltpu.VMEM((PEER_SEL, 2 * ROW_TILES, LANES), uv.dtype) for _ in range(nbuf)]
            + [pltpu.SemaphoreType.DMA((nbuf,)),
               pltpu.VMEM((2, PEER_SEL, LANES), F32),
               pltpu.VMEM((2, PEER_SEL, LANES), F32),
               pltpu.VMEM((2, SUBLANES, PEER_SEL), F32)]),
        compiler_params=_params(("arbitrary",)),
        name="peer_experts",
    )(eidx, gate, x3, s2, sh2, uv)


def _res_ln_kernel(x_ref, f_ref, g_ref, lg_ref, lb_ref, o_ref, *, alpha):
    y = alpha * x_ref[0] + g_ref[0] * f_ref[0]
    o_ref[0] = _layer_norm(y, lg_ref[...], lb_ref[...])


def _res_ln(x, f, g, lg, lb, alpha):
    b, seq, d = x.shape
    ts = TOKEN_TILE
    tok = pl.BlockSpec((1, ts, d), lambda bi, i: (bi, i, 0))
    row = pl.BlockSpec((1, d), lambda bi, i: (0, 0))
    return pl.pallas_call(
        functools.partial(_res_ln_kernel, alpha=alpha),
        grid=(b, seq // ts),
        in_specs=[tok, tok, pl.BlockSpec((1, 1, d), lambda bi, i: (bi, 0, 0)), row, row],
        out_specs=tok,
        out_shape=jax.ShapeDtypeStruct(x.shape, F32),
        compiler_params=_params(("arbitrary", "arbitrary")),
        name="res_ln",
    )(x, f, g, lg, lb)


def _trunk(x, mod, w):
    b, seq, d = x.shape
    depth = mod.shape[0]
    alpha = float((2 * depth) ** 0.25)
    for i in range(depth):
        sh1, s1, g1, sh2, s2, g2 = [m.reshape(b, 1, d) for m in jnp.split(mod[i], 6, axis=-1)]
        lw = w[i]
        if i % 2 == 0:
            x = _pool_layer(x, s1, sh1, g1, lw["pool_w"], lw["pool_scale"], lw["wo"],
                            lw["ln1_g"], lw["ln1_b"], alpha)
        else:
            q, k, v = _qkv(x, s1, sh1, lw["wqkv"])
            x = _attn_layer(x, q, k, v, g1, lw["sink"], lw["wo"], lw["ln1_g"], lw["ln1_b"], alpha)
        eidx, gate = _route(x, s2, sh2, lw["peer_wqt"], lw["peer_keys"])
        f3 = _experts(eidx.reshape(b * seq, PEER_SEL), gate.reshape(b * seq, PEER_SEL),
                      x.reshape(b * seq, ROW_TILES, LANES),
                      s2.reshape(b, ROW_TILES, LANES), sh2.reshape(b, ROW_TILES, LANES),
                      lw["uv"], seq)
        x = _res_ln(x, f3.reshape(b, seq, d), g2, lw["ln2_g"], lw["ln2_b"], alpha)
    return x


def kernel(x_prompt, x_sample, c_prompt, c_sample, ada_w, ada_b, ln1_g, ln1_b, ln2_g, ln2_b,
           pool_w, pool_scale, pool_wo, attn_wqkv, attn_sink, attn_wo,
           peer_wq, peer_keys, peer_u, peer_v):
    depth, d = ln1_g.shape
    n_exp = peer_u.shape[1]
    bp, bs = c_prompt.shape[0], c_sample.shape[0]
    m_pad = -(bp + bs) % SUBLANES
    c_all = jnp.concatenate([c_prompt, c_sample, jnp.zeros((m_pad, d), F32)], axis=0)
    mod = _ada(c_all, ada_w, ada_b)

    layers = []
    for i in range(depth):
        j = i // 2
        lw = {
            "ln1_g": ln1_g[i].reshape(1, d), "ln1_b": ln1_b[i].reshape(1, d),
            "ln2_g": ln2_g[i].reshape(1, d), "ln2_b": ln2_b[i].reshape(1, d),
            "peer_wqt": peer_wq[i].T.astype(BF16),
            "peer_keys": peer_keys[i].astype(BF16),
            "uv": jnp.concatenate([peer_u[i].astype(BF16), peer_v[i].astype(BF16)],
                                  axis=-1).reshape(n_exp, 2 * ROW_TILES, LANES),
        }
        if i % 2 == 0:
            lw.update(pool_w=pool_w[j].astype(BF16), pool_scale=pool_scale[j].reshape(1, d),
                      wo=pool_wo[j].astype(BF16))
        else:
            lw.update(wqkv=attn_wqkv[j].astype(BF16), sink=attn_sink[j],
                      wo=attn_wo[j].astype(BF16))
        layers.append(lw)

    y_prompt = _trunk(x_prompt, mod[:, :bp], layers)
    y_sample = _trunk(x_sample, mod[:, bp:bp + bs], layers)
    return (y_prompt, y_sample)
```

```python
import functools

import numpy as np
import jax
import jax.numpy as jnp
from jax import lax
from jax.experimental import pallas as pl
from jax.experimental.pallas import tpu as pltpu

F32 = jnp.float32
BF16 = jnp.bfloat16

D_MODEL = 2048
POOL_WINDOWS = (2, 4, 8, 16)
POOL_CH = D_MODEL // len(POOL_WINDOWS)
POOL_HALO = 8
HEAD_DIM = 128
N_HEADS = D_MODEL // HEAD_DIM
N_KV_HEADS = 4
GQA_GROUP = N_HEADS // N_KV_HEADS
WINDOW = 128
BLOCK = 128
Q_DIM = N_HEADS * HEAD_DIM
KV_DIM = N_KV_HEADS * HEAD_DIM
N_KEYS = 128
PEER_HEADS = 8
PEER_TOPK = 16
PEER_SEL = PEER_HEADS * PEER_TOPK
LN_EPS = 1e-5
NEG = -1e30

LANES = 128
SUBLANES = 8
ROW_TILES = D_MODEL // LANES
VMEM_LIMIT = 52 * 1024 * 1024

ADA_TN = 1024
TOKEN_TILE = 256
ROUTE_TILE = 256
PEER_TB = 128
PEER_NBUF = 16
PEER_AHEAD = 12


def _alibi_slopes():
    return [2.0 ** (-8.0 * (h + 1) / N_HEADS) for h in range(N_HEADS)]


def _params(sem):
    return pltpu.CompilerParams(dimension_semantics=sem, vmem_limit_bytes=VMEM_LIMIT)


def _layer_norm(y, g, b):
    mu = jnp.mean(y, axis=-1, keepdims=True)
    d = y - mu
    var = jnp.mean(d * d, axis=-1, keepdims=True)
    return d * lax.rsqrt(var + LN_EPS) * g + b


def _ada_kernel(c_ref, w_ref, b_ref, o_ref):
    c = c_ref[...]
    sc = c * jax.nn.sigmoid(c)
    o_ref[0] = jnp.dot(sc.astype(BF16), w_ref[0].astype(BF16),
                       preferred_element_type=F32) + b_ref[0]


def _ada(c_all, ada_w, ada_b):
    depth, d, n = ada_w.shape
    m = c_all.shape[0]
    return pl.pallas_call(
        _ada_kernel,
        grid=(depth, n // ADA_TN),
        in_specs=[
            pl.BlockSpec((m, d), lambda l, j: (0, 0)),
            pl.BlockSpec((1, d, ADA_TN), lambda l, j: (l, 0, j)),
            pl.BlockSpec((1, 1, ADA_TN), lambda l, j: (l, 0, j)),
        ],
        out_specs=pl.BlockSpec((1, m, ADA_TN), lambda l, j: (l, 0, j)),
        out_shape=jax.ShapeDtypeStruct((depth, m, n), F32),
        compiler_params=_params(("arbitrary", "arbitrary")),
        name="ada_mod",
    )(c_all, ada_w, ada_b.reshape(depth, 1, n))


def _pool_kernel(xp_ref, x_ref, xn_ref, s_ref, sh_ref, g_ref, wg_ref, ps_ref, wo_ref,
                 lg_ref, lb_ref, o_ref, *, seq, ts, alpha):
    i = pl.program_id(1)
    n_i = pl.num_programs(1)
    s1 = 1.0 + s_ref[0]
    sh = sh_ref[0]
    x = x_ref[0]
    h = x * s1 + sh
    hp = jnp.where(i > 0, xp_ref[0] * s1 + sh, 0.0)
    hn = jnp.where(i < n_i - 1, xn_ref[0] * s1 + sh, 0.0)
    ext = jnp.concatenate([hp, h, hn], axis=0)
    n_ext = ts + 2 * POOL_HALO
    pos = i * ts + lax.broadcasted_iota(jnp.int32, (ts, 1), 0)

    def shifted(a, k):
        return pltpu.roll(a, k % n_ext, axis=0)

    f = None
    for g, w in enumerate(POOL_WINDOWS):
        cs = slice(g * POOL_CH, (g + 1) * POOL_CH)
        e = ext[:, cs]
        acc = e + shifted(e, 1)
        half = 1
        while 2 * half < w:
            acc = shifted(acc, half) + shifted(acc, -half)
            half *= 2
        win = acc[POOL_HALO:POOL_HALO + ts]
        cnt = (jnp.minimum(pos + w // 2, seq) - jnp.maximum(pos - w // 2, 0)).astype(F32)
        pooled = win / cnt - h[:, cs]
        mixed = jnp.dot(pooled.astype(BF16), wg_ref[g], preferred_element_type=F32)
        mixed = mixed * ps_ref[:, cs]
        part = jnp.dot(mixed.astype(BF16), wo_ref[cs, :], preferred_element_type=F32)
        f = part if f is None else f + part
    y = alpha * x + g_ref[0] * f
    o_ref[0] = _layer_norm(y, lg_ref[...], lb_ref[...])


def _pool_layer(x, s1, sh1, g1, wg, ps, wo, lg, lb, alpha):
    b, seq, d = x.shape
    ts = TOKEN_TILE
    per = ts // POOL_HALO
    n_halo = seq // POOL_HALO
    vec = pl.BlockSpec((1, 1, d), lambda bi, i: (bi, 0, 0))
    row = pl.BlockSpec((1, d), lambda bi, i: (0, 0))
    return pl.pallas_call(
        functools.partial(_pool_kernel, seq=seq, ts=ts, alpha=alpha),
        grid=(b, seq // ts),
        in_specs=[
            pl.BlockSpec((1, POOL_HALO, d), lambda bi, i: (bi, jnp.maximum(i * per - 1, 0), 0)),
            pl.BlockSpec((1, ts, d), lambda bi, i: (bi, i, 0)),
            pl.BlockSpec((1, POOL_HALO, d),
                         lambda bi, i: (bi, jnp.minimum((i + 1) * per, n_halo - 1), 0)),
            vec, vec, vec,
            pl.BlockSpec(wg.shape, lambda bi, i: (0, 0, 0)),
            row,
            pl.BlockSpec((d, d), lambda bi, i: (0, 0)),
            row, row,
        ],
        out_specs=pl.BlockSpec((1, ts, d), lambda bi, i: (bi, i, 0)),
        out_shape=jax.ShapeDtypeStruct((b, seq, d), F32),
        compiler_params=_params(("arbitrary", "arbitrary")),
        name="pool_layer",
    )(x, x, x, s1, sh1, g1, wg, ps, wo, lg, lb)


def _qkv_kernel(x_ref, s_ref, sh_ref, w_ref, q_ref, k_ref, v_ref):
    h = x_ref[0] * (1.0 + s_ref[0]) + sh_ref[0]
    qkv = jnp.dot(h.astype(BF16), w_ref[...], preferred_element_type=F32)
    q_ref[0] = qkv[:, :Q_DIM].astype(BF16)
    k_ref[0] = qkv[:, Q_DIM:Q_DIM + KV_DIM].astype(BF16)
    v_ref[0] = qkv[:, Q_DIM + KV_DIM:].astype(BF16)


def _qkv(x, s1, sh1, w):
    b, seq, d = x.shape
    ts = TOKEN_TILE
    vec = pl.BlockSpec((1, 1, d), lambda bi, i: (bi, 0, 0))
    tok = lambda n: pl.BlockSpec((1, ts, n), lambda bi, i: (bi, i, 0))
    return pl.pallas_call(
        _qkv_kernel,
        grid=(b, seq // ts),
        in_specs=[tok(d), vec, vec, pl.BlockSpec(w.shape, lambda bi, i: (0, 0))],
        out_specs=[tok(Q_DIM), tok(KV_DIM), tok(KV_DIM)],
        out_shape=[jax.ShapeDtypeStruct((b, seq, Q_DIM), BF16),
                   jax.ShapeDtypeStruct((b, seq, KV_DIM), BF16),
                   jax.ShapeDtypeStruct((b, seq, KV_DIM), BF16)],
        compiler_params=_params(("arbitrary", "arbitrary")),
        name="attn_qkv",
    )(x, s1, sh1, w)


def _attn_kernel(sink_ref, q_ref, kp_ref, kc_ref, kn_ref, vp_ref, vc_ref, vn_ref, x_ref, g_ref,
                 wo_ref, lg_ref, lb_ref, o_ref, att_ref, *, seq, tq, alpha):
    i = pl.program_id(1)
    kcat = jnp.concatenate([kp_ref[0], kc_ref[0], kn_ref[0]], axis=0)
    vcat = jnp.concatenate([vp_ref[0], vc_ref[0], vn_ref[0]], axis=0)
    qi = lax.broadcasted_iota(jnp.int32, (BLOCK, 3 * BLOCK), 0)
    ki = lax.broadcasted_iota(jnp.int32, (BLOCK, 3 * BLOCK), 1)
    rel = jnp.abs(ki - BLOCK - qi)
    relf = rel.astype(F32)
    scale = HEAD_DIM ** -0.5
    slopes = _alibi_slopes()
    for j in range(tq // BLOCK):
        kpos = i * tq + (j - 1) * BLOCK + ki
        valid = (rel <= WINDOW) & (kpos >= 0) & (kpos < seq)
        rows = slice(j * BLOCK, (j + 1) * BLOCK)
        for hd in range(N_HEADS):
            hk = hd // GQA_GROUP
            qh = q_ref[0, rows, hd * HEAD_DIM:(hd + 1) * HEAD_DIM]
            kh = kcat[j * BLOCK:(j + 3) * BLOCK, hk * HEAD_DIM:(hk + 1) * HEAD_DIM]
            vh = vcat[j * BLOCK:(j + 3) * BLOCK, hk * HEAD_DIM:(hk + 1) * HEAD_DIM]
            s = lax.dot_general(qh, kh, (((1,), (1,)), ((), ())), preferred_element_type=F32)
            s = jnp.where(valid, s * scale - slopes[hd] * relf, NEG)
            sink = sink_ref[hd]
            m = jnp.maximum(jnp.max(s, axis=-1, keepdims=True), sink)
            p = jnp.exp(s - m)
            p = p / (jnp.sum(p, axis=-1, keepdims=True) + jnp.exp(sink - m))
            o = jnp.dot(p.astype(BF16), vh, preferred_element_type=F32)
            att_ref[rows, hd * HEAD_DIM:(hd + 1) * HEAD_DIM] = o.astype(BF16)
    f = jnp.dot(att_ref[...], wo_ref[...], preferred_element_type=F32)
    y = alpha * x_ref[0] + g_ref[0] * f
    o_ref[0] = _layer_norm(y, lg_ref[...], lb_ref[...])


def _attn_layer(x, q, k, v, g1, sink, wo, lg, lb, alpha):
    b, seq, d = x.shape
    tq = TOKEN_TILE
    per = tq // BLOCK
    nb = seq // BLOCK
    vec = pl.BlockSpec((1, 1, d), lambda bi, i: (bi, 0, 0))
    row = pl.BlockSpec((1, d), lambda bi, i: (0, 0))
    prev = pl.BlockSpec((1, BLOCK, KV_DIM), lambda bi, i: (bi, jnp.maximum(i * per - 1, 0), 0))
    cur = pl.BlockSpec((1, tq, KV_DIM), lambda bi, i: (bi, i, 0))
    nxt = pl.BlockSpec((1, BLOCK, KV_DIM),
                       lambda bi, i: (bi, jnp.minimum((i + 1) * per, nb - 1), 0))
    return pl.pallas_call(
        functools.partial(_attn_kernel, seq=seq, tq=tq, alpha=alpha),
        grid=(b, seq // tq),
        in_specs=[
            pl.BlockSpec(memory_space=pltpu.SMEM),
            pl.BlockSpec((1, tq, Q_DIM), lambda bi, i: (bi, i, 0)),
            prev, cur, nxt, prev, cur, nxt,
            pl.BlockSpec((1, tq, d), lambda bi, i: (bi, i, 0)),
            vec,
            pl.BlockSpec((d, d), lambda bi, i: (0, 0)),
            row, row,
        ],
        out_specs=pl.BlockSpec((1, tq, d), lambda bi, i: (bi, i, 0)),
        out_shape=jax.ShapeDtypeStruct((b, seq, d), F32),
        scratch_shapes=[pltpu.VMEM((tq, d), BF16)],
        compiler_params=_params(("arbitrary", "arbitrary")),
        name="attn_layer",
    )(sink, q, k, k, k, v, v, v, x, g1, wo, lg, lb)


def _top16_rows(scores, payloads):
    slot = lax.broadcasted_iota(jnp.int32, (PEER_TOPK, LANES), 0)
    rows = [lax.broadcasted_iota(jnp.int32, s.shape, 0).astype(F32) for s in scores]
    zero = jnp.zeros((PEER_TOPK, LANES), F32)

    def rnd(r, carry):
        out = []
        for (s, v, p), row, pay in zip(carry, rows, payloads):
            m = jnp.max(s, axis=0, keepdims=True)
            am = jnp.min(jnp.where(s == m, row, float(s.shape[0])), axis=0, keepdims=True)
            hit = row == am
            sel = am if pay is None else jnp.max(jnp.where(hit, pay, -1.0), axis=0, keepdims=True)
            out.append((jnp.where(hit, -jnp.inf, s), jnp.where(slot == r, m, v),
                        jnp.where(slot == r, sel, p)))
        return tuple(out)

    done = lax.fori_loop(0, PEER_TOPK, rnd, tuple((s, zero, zero) for s in scores))
    return [(v, p) for _, v, p in done]


CAND_ROWS = 64


def _candidate_tables():
    pairs = [(i, j) for i in range(PEER_TOPK) for j in range(PEER_TOPK)
             if (i + 1) * (j + 1) <= PEER_TOPK]
    e1 = np.zeros((CAND_ROWS, LANES), np.float32)
    e2 = np.zeros((CAND_ROWS, LANES), np.float32)
    for r, (i, j) in enumerate(pairs):
        e1[r, i] = 1.0
        e2[r, j] = 1.0
    return e1, e2, len(pairs)


def _route_kernel(x_ref, s_ref, sh_ref, wqt_ref, keys_ref, e1_ref, e2_ref, eidx_ref, gate_ref,
                  qt_ref, ei_ref, g_ref, *, tr, n_cand):
    n_lt = tr // LANES
    h = x_ref[0] * (1.0 + s_ref[0]) + sh_ref[0]
    qt = lax.dot_general(wqt_ref[...], h.astype(BF16), (((1,), (1,)), ((), ())),
                         preferred_element_type=F32)
    for lt in range(n_lt):
        for hc in range(2 * PEER_HEADS):
            qt_ref[lt, hc] = qt[hc * N_KEYS:(hc + 1) * N_KEYS,
                                lt * LANES:(lt + 1) * LANES].astype(BF16)
    cand_row = lax.broadcasted_iota(jnp.int32, (CAND_ROWS, LANES), 0)
    pad = jnp.zeros((LANES - PEER_TOPK, LANES), F32)
    hi = lax.Precision.HIGHEST

    def pick(sel_ref, top, exact_in_bf16=False):
        full = jnp.concatenate([top, pad], axis=0)
        if exact_in_bf16:
            return jnp.dot(sel_ref[...].astype(BF16), full.astype(BF16),
                           preferred_element_type=F32)
        return jnp.dot(sel_ref[...], full, precision=hi, preferred_element_type=F32)

    def job(hh, lt):
        sc = [jnp.dot(keys_ref[hh, c], qt_ref[lt, hh * 2 + c], preferred_element_type=F32)
              for c in range(2)]
        (v1, i1), (v2, i2) = _top16_rows(sc, [None, None])
        cand = jnp.where(cand_row < n_cand, pick(e1_ref, v1) + pick(e2_ref, v2), -jnp.inf)
        cidx = pick(e1_ref, i1, True) * float(N_KEYS) + pick(e2_ref, i2, True)
        ((best, eid),) = _top16_rows([cand], [cidx])
        ex = jnp.exp(best - best[0:1, :])
        gate = ex / jnp.sum(ex, axis=0, keepdims=True)
        dst = pl.ds(pl.multiple_of(hh * PEER_TOPK, PEER_TOPK), PEER_TOPK)
        ei_ref[lt, dst, :] = eid.astype(jnp.int32)
        g_ref[lt, dst, :] = gate

    def head(hh, carry):
        for lt in range(n_lt):
            job(hh, lt)
        return carry

    lax.fori_loop(0, PEER_HEADS, head, 0)
    for lt in range(n_lt):
        eidx_ref[0, lt * LANES:(lt + 1) * LANES, :] = ei_ref[lt].T
        gate_ref[0, lt * LANES:(lt + 1) * LANES, :] = g_ref[lt].T


def _route(x, s2, sh2, wqt, keys):
    b, seq, d = x.shape
    tr = ROUTE_TILE
    n_lt = tr // LANES
    e1, e2, n_cand = _candidate_tables()
    vec = pl.BlockSpec((1, 1, d), lambda bi, i: (bi, 0, 0))
    sel = pl.BlockSpec((CAND_ROWS, LANES), lambda bi, i: (0, 0))
    out = pl.BlockSpec((1, tr, PEER_SEL), lambda bi, i: (bi, i, 0))
    return pl.pallas_call(
        functools.partial(_route_kernel, tr=tr, n_cand=n_cand),
        grid=(b, seq // tr),
        in_specs=[
            pl.BlockSpec((1, tr, d), lambda bi, i: (bi, i, 0)),
            vec, vec,
            pl.BlockSpec(wqt.shape, lambda bi, i: (0, 0)),
            pl.BlockSpec(keys.shape, lambda bi, i: (0, 0, 0, 0)),
            sel, sel,
        ],
        out_specs=[out, out],
        out_shape=[jax.ShapeDtypeStruct((b, seq, PEER_SEL), jnp.int32),
                   jax.ShapeDtypeStruct((b, seq, PEER_SEL), F32)],
        scratch_shapes=[pltpu.VMEM((n_lt, 2 * PEER_HEADS, N_KEYS, LANES), BF16),
                        pltpu.VMEM((n_lt, PEER_SEL, LANES), jnp.int32),
                        pltpu.VMEM((n_lt, PEER_SEL, LANES), F32)],
        compiler_params=_params(("arbitrary", "arbitrary")),
        name="peer_route",
    )(x, s2, sh2, wqt, keys, jnp.asarray(e1), jnp.asarray(e2))


def _row_sums8(ps, sub):
    def merge(a, b, d):
        va = a + pltpu.roll(a, SUBLANES - d, axis=0)
        vb = b + pltpu.roll(b, d, axis=0)
        return jnp.where((sub & d) == 0, va, vb)
    y = [merge(ps[i], ps[i + 4], 4) for i in range(4)]
    x0 = merge(y[0], y[2], 2)
    x1 = merge(y[1], y[3], 2)
    return merge(x0, x1, 1)


def _expert_kernel(idx_ref, gate_ref, x_ref, s_ref, sh_ref, uv_ref, o_ref, *scratch, tb, nbuf):
    bufs = scratch[:nbuf]
    sem = scratch[nbuf]
    wb_ref = scratch[nbuf + 1]
    q_ref = scratch[nbuf + 2]
    w_ref = scratch[nbuf + 3]
    ahead = PEER_AHEAD
    assert 3 < ahead < nbuf
    s2 = 1.0 + s_ref[0]
    sh = sh_ref[0]
    sub = lax.broadcasted_iota(jnp.int32, (SUBLANES, LANES), 0)

    n_chunks = PEER_SEL // SUBLANES

    def issue_chunk(t, slot, c):
        for e in range(c * SUBLANES, (c + 1) * SUBLANES):
            pltpu.make_async_copy(uv_ref.at[idx_ref[t, e]], bufs[slot].at[e],
                                  sem.at[slot]).start(priority=e % 2)

    def wait(slot):
        pltpu.make_async_copy(uv_ref.at[pl.ds(0, PEER_SEL)], bufs[slot], sem.at[slot]).wait()

    def dots_chunk(xt, slot, c):
        ps = []
        for k in range(SUBLANES):
            p = bufs[slot][c * SUBLANES + k, 0:ROW_TILES, :].astype(F32) * xt
            ps.append(p[0:SUBLANES] + p[SUBLANES:2 * SUBLANES])
        return _row_sums8(ps, sub)

    def gated(t, par):
        a = jnp.sum(q_ref[par].T, axis=0, keepdims=True)
        act = 0.5 * a * (1.0 + lax.erf(a * np.float32(np.sqrt(0.5))))
        w_ref[par] = jnp.broadcast_to(act * gate_ref[pl.ds(t, 1), :], (SUBLANES, PEER_SEL))

    def spread(par):
        w = w_ref[par, 0:1, :]
        wb_ref[par] = jnp.broadcast_to(w, (LANES, PEER_SEL)).T

    def combine_chunk(accs, slot, c):
        for e in range(c * SUBLANES, (c + 1) * SUBLANES):
            term = (wb_ref[slot % 2, e:e + 1, :]
                    * bufs[slot][e, ROW_TILES:2 * ROW_TILES, :].astype(F32))
            accs[e % 4] = term if accs[e % 4] is None else accs[e % 4] + term

    def step(t, k, t_static=None):
        live = lambda d: t_static is None or 0 <= t_static + d < tb
        slot_i, slot_d = (k + ahead) % nbuf, (k + 3) % nbuf
        if live(3):
            wait(slot_d)
            xt = x_ref[t + 3] * s2 + sh
        if live(2):
            gated(t + 2, k % 2)
        if live(1):
            spread((k + 1) % 2)
        accs = [None] * 4
        for c in range(n_chunks):
            if live(ahead):
                issue_chunk(t + ahead, slot_i, c)
            if live(3):
                q_ref[(k + 1) % 2, c * SUBLANES:(c + 1) * SUBLANES, :] = dots_chunk(xt, slot_d, c)
            if live(0):
                combine_chunk(accs, k, c)
        if live(0):
            o_ref[t] = (accs[0] + accs[1]) + (accs[2] + accs[3])

    for t in range(ahead - 3):
        for c in range(n_chunks):
            issue_chunk(t, t, c)
    for t in range(-3, 0):
        step(t, t % nbuf, t)

    def group(gi, carry):
        for k in range(nbuf):
            step(gi * nbuf + k, k)
        return carry

    n_groups = tb // nbuf
    lax.fori_loop(0, n_groups - 1, group, 0)
    for k in range(nbuf):
        t = (n_groups - 1) * nbuf + k
        step(t, k, t)


def _experts(eidx, gate, x3, s2, sh2, uv, seq):
    t_all = x3.shape[0]
    tb, nbuf = PEER_TB, PEER_NBUF
    per_seq = seq // tb
    tok = pl.BlockSpec((tb, ROW_TILES, LANES), lambda i: (i, 0, 0))
    vec = pl.BlockSpec((1, ROW_TILES, LANES), lambda i: (i // per_seq, 0, 0))
    return pl.pallas_call(
        functools.partial(_expert_kernel, tb=tb, nbuf=nbuf),
        grid=(t_all // tb,),
        in_specs=[
            pl.BlockSpec((tb, PEER_SEL), lambda i: (i, 0), memory_space=pltpu.SMEM),
            pl.BlockSpec((tb, PEER_SEL), lambda i: (i, 0)),
            tok, vec, vec,
            pl.BlockSpec(memory_space=pl.ANY),
        ],
        out_specs=tok,
        out_shape=jax.ShapeDtypeStruct(x3.shape, F32),
        scratch_shapes=(
            [pltpu.VMEM((PEER_SEL, 2 * ROW_TILES, LANES), uv.dtype) for _ in range(nbuf)]
            + [pltpu.SemaphoreType.DMA((nbuf,)),
               pltpu.VMEM((2, PEER_SEL, LANES), F32),
               pltpu.VMEM((2, PEER_SEL, LANES), F32),
               pltpu.VMEM((2, SUBLANES, PEER_SEL), F32)]),
        compiler_params=_params(("arbitrary",)),
        name="peer_experts",
    )(eidx, gate, x3, s2, sh2, uv)


def _res_ln_kernel(x_ref, f_ref, g_ref, lg_ref, lb_ref, o_ref, *, alpha):
    y = alpha * x_ref[0] + g_ref[0] * f_ref[0]
    o_ref[0] = _layer_norm(y, lg_ref[...], lb_ref[...])


def _res_ln(x, f, g, lg, lb, alpha):
    b, seq, d = x.shape
    ts = TOKEN_TILE
    tok = pl.BlockSpec((1, ts, d), lambda bi, i: (bi, i, 0))
    row = pl.BlockSpec((1, d), lambda bi, i: (0, 0))
    return pl.pallas_call(
        functools.partial(_res_ln_kernel, alpha=alpha),
        grid=(b, seq // ts),
        in_specs=[tok, tok, pl.BlockSpec((1, 1, d), lambda bi, i: (bi, 0, 0)), row, row],
        out_specs=tok,
        out_shape=jax.ShapeDtypeStruct(x.shape, F32),
        compiler_params=_params(("arbitrary", "arbitrary")),
        name="res_ln",
    )(x, f, g, lg, lb)


def _trunk(x, mod, w):
    b, seq, d = x.shape
    depth = mod.shape[0]
    alpha = float((2 * depth) ** 0.25)
    for i in range(depth):
        sh1, s1, g1, sh2, s2, g2 = [m.reshape(b, 1, d) for m in jnp.split(mod[i], 6, axis=-1)]
        lw = w[i]
        if i % 2 == 0:
            x = _pool_layer(x, s1, sh1, g1, lw["pool_w"], lw["pool_scale"], lw["wo"],
                            lw["ln1_g"], lw["ln1_b"], alpha)
        else:
            q, k, v = _qkv(x, s1, sh1, lw["wqkv"])
            x = _attn_layer(x, q, k, v, g1, lw["sink"], lw["wo"], lw["ln1_g"], lw["ln1_b"], alpha)
        eidx, gate = _route(x, s2, sh2, lw["peer_wqt"], lw["peer_keys"])
        f3 = _experts(eidx.reshape(b * seq, PEER_SEL), gate.reshape(b * seq, PEER_SEL),
                      x.reshape(b * seq, ROW_TILES, LANES),
                      s2.reshape(b, ROW_TILES, LANES), sh2.reshape(b, ROW_TILES, LANES),
                      lw["uv"], seq)
        x = _res_ln(x, f3.reshape(b, seq, d), g2, lw["ln2_g"], lw["ln2_b"], alpha)
    return x


def kernel(x_prompt, x_sample, c_prompt, c_sample, ada_w, ada_b, ln1_g, ln1_b, ln2_g, ln2_b,
           pool_w, pool_scale, pool_wo, attn_wqkv, attn_sink, attn_wo,
           peer_wq, peer_keys, peer_u, peer_v):
    depth, d = ln1_g.shape
    n_exp = peer_u.shape[1]
    bp, bs = c_prompt.shape[0], c_sample.shape[0]
    m_pad = -(bp + bs) % SUBLANES
    c_all = jnp.concatenate([c_prompt, c_sample, jnp.zeros((m_pad, d), F32)], axis=0)
    mod = _ada(c_all, ada_w, ada_b)

    layers = []
    for i in range(depth):
        j = i // 2
        lw = {
            "ln1_g": ln1_g[i].reshape(1, d), "ln1_b": ln1_b[i].reshape(1, d),
            "ln2_g": ln2_g[i].reshape(1, d), "ln2_b": ln2_b[i].reshape(1, d),
            "peer_wqt": peer_wq[i].T.astype(BF16),
            "peer_keys": peer_keys[i].astype(BF16),
            "uv": jnp.concatenate([peer_u[i].astype(BF16), peer_v[i].astype(BF16)],
                                  axis=-1).reshape(n_exp, 2 * ROW_TILES, LANES),
        }
        if i % 2 == 0:
            lw.update(pool_w=pool_w[j].astype(BF16), pool_scale=pool_scale[j].reshape(1, d),
                      wo=pool_wo[j].astype(BF16))
        else:
            lw.update(wqkv=attn_wqkv[j].astype(BF16), sink=attn_sink[j],
                      wo=attn_wo[j].astype(BF16))
        layers.append(lw)

    y_prompt = _trunk(x_prompt, mod[:, :bp], layers)
    y_sample = _trunk(x_sample, mod[:, bp:bp + bs], layers)
    return (y_prompt, y_sample)
```

```python
import functools

import numpy as np
import jax
import jax.numpy as jnp
from jax import lax
from jax.experimental import pallas as pl
from jax.experimental.pallas import tpu as pltpu

F32 = jnp.float32
BF16 = jnp.bfloat16

D_MODEL = 2048
POOL_WINDOWS = (2, 4, 8, 16)
POOL_CH = D_MODEL // len(POOL_WINDOWS)
POOL_HALO = 8
HEAD_DIM = 128
N_HEADS = D_MODEL // HEAD_DIM
N_KV_HEADS = 4
GQA_GROUP = N_HEADS // N_KV_HEADS
WINDOW = 128
BLOCK = 128
Q_DIM = N_HEADS * HEAD_DIM
KV_DIM = N_KV_HEADS * HEAD_DIM
N_KEYS = 128
PEER_HEADS = 8
PEER_TOPK = 16
PEER_SEL = PEER_HEADS * PEER_TOPK
LN_EPS = 1e-5
NEG = -1e30

LANES = 128
SUBLANES = 8
ROW_TILES = D_MODEL // LANES
VMEM_LIMIT = 52 * 1024 * 1024

ADA_TN = 1024
TOKEN_TILE = 256
ROUTE_TILE = 256
PEER_TB = 256
PEER_NBUF = 16
PEER_AHEAD = 12


def _alibi_slopes():
    return [2.0 ** (-8.0 * (h + 1) / N_HEADS) for h in range(N_HEADS)]


def _params(sem):
    return pltpu.CompilerParams(dimension_semantics=sem, vmem_limit_bytes=VMEM_LIMIT)


def _layer_norm(y, g, b):
    mu = jnp.mean(y, axis=-1, keepdims=True)
    d = y - mu
    var = jnp.mean(d * d, axis=-1, keepdims=True)
    return d * lax.rsqrt(var + LN_EPS) * g + b


def _ada_kernel(c_ref, w_ref, b_ref, o_ref):
    c = c_ref[...]
    sc = c * jax.nn.sigmoid(c)
    o_ref[0] = jnp.dot(sc.astype(BF16), w_ref[0].astype(BF16),
                       preferred_element_type=F32) + b_ref[0]


def _ada(c_all, ada_w, ada_b):
    depth, d, n = ada_w.shape
    m = c_all.shape[0]
    return pl.pallas_call(
        _ada_kernel,
        grid=(depth, n // ADA_TN),
        in_specs=[
            pl.BlockSpec((m, d), lambda l, j: (0, 0)),
            pl.BlockSpec((1, d, ADA_TN), lambda l, j: (l, 0, j)),
            pl.BlockSpec((1, 1, ADA_TN), lambda l, j: (l, 0, j)),
        ],
        out_specs=pl.BlockSpec((1, m, ADA_TN), lambda l, j: (l, 0, j)),
        out_shape=jax.ShapeDtypeStruct((depth, m, n), F32),
        compiler_params=_params(("arbitrary", "arbitrary")),
        name="ada_mod",
    )(c_all, ada_w, ada_b.reshape(depth, 1, n))


def _pool_kernel(xp_ref, x_ref, xn_ref, s_ref, sh_ref, g_ref, wg_ref, ps_ref, wo_ref,
                 lg_ref, lb_ref, o_ref, *, seq, ts, alpha):
    i = pl.program_id(1)
    n_i = pl.num_programs(1)
    s1 = 1.0 + s_ref[0]
    sh = sh_ref[0]
    x = x_ref[0]
    h = x * s1 + sh
    hp = jnp.where(i > 0, xp_ref[0] * s1 + sh, 0.0)
    hn = jnp.where(i < n_i - 1, xn_ref[0] * s1 + sh, 0.0)
    ext = jnp.concatenate([hp, h, hn], axis=0)
    n_ext = ts + 2 * POOL_HALO
    pos = i * ts + lax.broadcasted_iota(jnp.int32, (ts, 1), 0)

    def shifted(a, k):
        return pltpu.roll(a, k % n_ext, axis=0)

    f = None
    for g, w in enumerate(POOL_WINDOWS):
        cs = slice(g * POOL_CH, (g + 1) * POOL_CH)
        e = ext[:, cs]
        acc = e + shifted(e, 1)
        half = 1
        while 2 * half < w:
            acc = shifted(acc, half) + shifted(acc, -half)
            half *= 2
        win = acc[POOL_HALO:POOL_HALO + ts]
        cnt = (jnp.minimum(pos + w // 2, seq) - jnp.maximum(pos - w // 2, 0)).astype(F32)
        pooled = win / cnt - h[:, cs]
        mixed = jnp.dot(pooled.astype(BF16), wg_ref[g], preferred_element_type=F32)
        mixed = mixed * ps_ref[:, cs]
        part = jnp.dot(mixed.astype(BF16), wo_ref[cs, :], preferred_element_type=F32)
        f = part if f is None else f + part
    y = alpha * x + g_ref[0] * f
    o_ref[0] = _layer_norm(y, lg_ref[...], lb_ref[...])


def _pool_layer(x, s1, sh1, g1, wg, ps, wo, lg, lb, alpha):
    b, seq, d = x.shape
    ts = TOKEN_TILE
    per = ts // POOL_HALO
    n_halo = seq // POOL_HALO
    vec = pl.BlockSpec((1, 1, d), lambda bi, i: (bi, 0, 0))
    row = pl.BlockSpec((1, d), lambda bi, i: (0, 0))
    return pl.pallas_call(
        functools.partial(_pool_kernel, seq=seq, ts=ts, alpha=alpha),
        grid=(b, seq // ts),
        in_specs=[
            pl.BlockSpec((1, POOL_HALO, d), lambda bi, i: (bi, jnp.maximum(i * per - 1, 0), 0)),
            pl.BlockSpec((1, ts, d), lambda bi, i: (bi, i, 0)),
            pl.BlockSpec((1, POOL_HALO, d),
                         lambda bi, i: (bi, jnp.minimum((i + 1) * per, n_halo - 1), 0)),
            vec, vec, vec,
            pl.BlockSpec(wg.shape, lambda bi, i: (0, 0, 0)),
            row,
            pl.BlockSpec((d, d), lambda bi, i: (0, 0)),
            row, row,
        ],
        out_specs=pl.BlockSpec((1, ts, d), lambda bi, i: (bi, i, 0)),
        out_shape=jax.ShapeDtypeStruct((b, seq, d), F32),
        compiler_params=_params(("arbitrary", "arbitrary")),
        name="pool_layer",
    )(x, x, x, s1, sh1, g1, wg, ps, wo, lg, lb)


def _qkv_kernel(x_ref, s_ref, sh_ref, w_ref, q_ref, k_ref, v_ref):
    h = x_ref[0] * (1.0 + s_ref[0]) + sh_ref[0]
    qkv = jnp.dot(h.astype(BF16), w_ref[...], preferred_element_type=F32)
    q_ref[0] = qkv[:, :Q_DIM].astype(BF16)
    k_ref[0] = qkv[:, Q_DIM:Q_DIM + KV_DIM].astype(BF16)
    v_ref[0] = qkv[:, Q_DIM + KV_DIM:].astype(BF16)


def _qkv(x, s1, sh1, w):
    b, seq, d = x.shape
    ts = TOKEN_TILE
    vec = pl.BlockSpec((1, 1, d), lambda bi, i: (bi, 0, 0))
    tok = lambda n: pl.BlockSpec((1, ts, n), lambda bi, i: (bi, i, 0))
    return pl.pallas_call(
        _qkv_kernel,
        grid=(b, seq // ts),
        in_specs=[tok(d), vec, vec, pl.BlockSpec(w.shape, lambda bi, i: (0, 0))],
        out_specs=[tok(Q_DIM), tok(KV_DIM), tok(KV_DIM)],
        out_shape=[jax.ShapeDtypeStruct((b, seq, Q_DIM), BF16),
                   jax.ShapeDtypeStruct((b, seq, KV_DIM), BF16),
                   jax.ShapeDtypeStruct((b, seq, KV_DIM), BF16)],
        compiler_params=_params(("arbitrary", "arbitrary")),
        name="attn_qkv",
    )(x, s1, sh1, w)


def _attn_kernel(sink_ref, q_ref, kp_ref, kc_ref, kn_ref, vp_ref, vc_ref, vn_ref, x_ref, g_ref,
                 wo_ref, lg_ref, lb_ref, o_ref, att_ref, *, seq, tq, alpha):
    i = pl.program_id(1)
    kcat = jnp.concatenate([kp_ref[0], kc_ref[0], kn_ref[0]], axis=0)
    vcat = jnp.concatenate([vp_ref[0], vc_ref[0], vn_ref[0]], axis=0)
    qi = lax.broadcasted_iota(jnp.int32, (BLOCK, 3 * BLOCK), 0)
    ki = lax.broadcasted_iota(jnp.int32, (BLOCK, 3 * BLOCK), 1)
    rel = jnp.abs(ki - BLOCK - qi)
    relf = rel.astype(F32)
    scale = HEAD_DIM ** -0.5
    slopes = _alibi_slopes()
    for j in range(tq // BLOCK):
        kpos = i * tq + (j - 1) * BLOCK + ki
        valid = (rel <= WINDOW) & (kpos >= 0) & (kpos < seq)
        rows = slice(j * BLOCK, (j + 1) * BLOCK)
        for hd in range(N_HEADS):
            hk = hd // GQA_GROUP
            qh = q_ref[0, rows, hd * HEAD_DIM:(hd + 1) * HEAD_DIM]
            kh = kcat[j * BLOCK:(j + 3) * BLOCK, hk * HEAD_DIM:(hk + 1) * HEAD_DIM]
            vh = vcat[j * BLOCK:(j + 3) * BLOCK, hk * HEAD_DIM:(hk + 1) * HEAD_DIM]
            s = lax.dot_general(qh, kh, (((1,), (1,)), ((), ())), preferred_element_type=F32)
            s = jnp.where(valid, s * scale - slopes[hd] * relf, NEG)
            sink = sink_ref[hd]
            m = jnp.maximum(jnp.max(s, axis=-1, keepdims=True), sink)
            p = jnp.exp(s - m)
            p = p / (jnp.sum(p, axis=-1, keepdims=True) + jnp.exp(sink - m))
            o = jnp.dot(p.astype(BF16), vh, preferred_element_type=F32)
            att_ref[rows, hd * HEAD_DIM:(hd + 1) * HEAD_DIM] = o.astype(BF16)
    f = jnp.dot(att_ref[...], wo_ref[...], preferred_element_type=F32)
    y = alpha * x_ref[0] + g_ref[0] * f
    o_ref[0] = _layer_norm(y, lg_ref[...], lb_ref[...])


def _attn_layer(x, q, k, v, g1, sink, wo, lg, lb, alpha):
    b, seq, d = x.shape
    tq = TOKEN_TILE
    per = tq // BLOCK
    nb = seq // BLOCK
    vec = pl.BlockSpec((1, 1, d), lambda bi, i: (bi, 0, 0))
    row = pl.BlockSpec((1, d), lambda bi, i: (0, 0))
    prev = pl.BlockSpec((1, BLOCK, KV_DIM), lambda bi, i: (bi, jnp.maximum(i * per - 1, 0), 0))
    cur = pl.BlockSpec((1, tq, KV_DIM), lambda bi, i: (bi, i, 0))
    nxt = pl.BlockSpec((1, BLOCK, KV_DIM),
                       lambda bi, i: (bi, jnp.minimum((i + 1) * per, nb - 1), 0))
    return pl.pallas_call(
        functools.partial(_attn_kernel, seq=seq, tq=tq, alpha=alpha),
        grid=(b, seq // tq),
        in_specs=[
            pl.BlockSpec(memory_space=pltpu.SMEM),
            pl.BlockSpec((1, tq, Q_DIM), lambda bi, i: (bi, i, 0)),
            prev, cur, nxt, prev, cur, nxt,
            pl.BlockSpec((1, tq, d), lambda bi, i: (bi, i, 0)),
            vec,
            pl.BlockSpec((d, d), lambda bi, i: (0, 0)),
            row, row,
        ],
        out_specs=pl.BlockSpec((1, tq, d), lambda bi, i: (bi, i, 0)),
        out_shape=jax.ShapeDtypeStruct((b, seq, d), F32),
        scratch_shapes=[pltpu.VMEM((tq, d), BF16)],
        compiler_params=_params(("arbitrary", "arbitrary")),
        name="attn_layer",
    )(sink, q, k, k, k, v, v, v, x, g1, wo, lg, lb)


def _top16_rows(scores, payloads):
    slot = lax.broadcasted_iota(jnp.int32, (PEER_TOPK, LANES), 0)
    rows = [lax.broadcasted_iota(jnp.int32, s.shape, 0).astype(F32) for s in scores]
    zero = jnp.zeros((PEER_TOPK, LANES), F32)

    def rnd(r, carry):
        out = []
        for (s, v, p), row, pay in zip(carry, rows, payloads):
            m = jnp.max(s, axis=0, keepdims=True)
            am = jnp.min(jnp.where(s == m, row, float(s.shape[0])), axis=0, keepdims=True)
            hit = row == am
            sel = am if pay is None else jnp.max(jnp.where(hit, pay, -1.0), axis=0, keepdims=True)
            out.append((jnp.where(hit, -jnp.inf, s), jnp.where(slot == r, m, v),
                        jnp.where(slot == r, sel, p)))
        return tuple(out)

    done = lax.fori_loop(0, PEER_TOPK, rnd, tuple((s, zero, zero) for s in scores))
    return [(v, p) for _, v, p in done]


CAND_ROWS = 64


def _candidate_tables():
    pairs = [(i, j) for i in range(PEER_TOPK) for j in range(PEER_TOPK)
             if (i + 1) * (j + 1) <= PEER_TOPK]
    e1 = np.zeros((CAND_ROWS, LANES), np.float32)
    e2 = np.zeros((CAND_ROWS, LANES), np.float32)
    for r, (i, j) in enumerate(pairs):
        e1[r, i] = 1.0
        e2[r, j] = 1.0
    return e1, e2, len(pairs)


def _route_kernel(x_ref, s_ref, sh_ref, wqt_ref, keys_ref, e1_ref, e2_ref, eidx_ref, gate_ref,
                  qt_ref, ei_ref, g_ref, *, tr, n_cand):
    n_lt = tr // LANES
    h = x_ref[0] * (1.0 + s_ref[0]) + sh_ref[0]
    qt = lax.dot_general(wqt_ref[...], h.astype(BF16), (((1,), (1,)), ((), ())),
                         preferred_element_type=F32)
    for lt in range(n_lt):
        for hc in range(2 * PEER_HEADS):
            qt_ref[lt, hc] = qt[hc * N_KEYS:(hc + 1) * N_KEYS,
                                lt * LANES:(lt + 1) * LANES].astype(BF16)
    cand_row = lax.broadcasted_iota(jnp.int32, (CAND_ROWS, LANES), 0)
    pad = jnp.zeros((LANES - PEER_TOPK, LANES), F32)
    hi = lax.Precision.HIGHEST

    def pick(sel_ref, top, exact_in_bf16=False):
        full = jnp.concatenate([top, pad], axis=0)
        if exact_in_bf16:
            return jnp.dot(sel_ref[...].astype(BF16), full.astype(BF16),
                           preferred_element_type=F32)
        return jnp.dot(sel_ref[...], full, precision=hi, preferred_element_type=F32)

    def job(hh, lt):
        sc = [jnp.dot(keys_ref[hh, c], qt_ref[lt, hh * 2 + c], preferred_element_type=F32)
              for c in range(2)]
        (v1, i1), (v2, i2) = _top16_rows(sc, [None, None])
        cand = jnp.where(cand_row < n_cand, pick(e1_ref, v1) + pick(e2_ref, v2), -jnp.inf)
        cidx = pick(e1_ref, i1, True) * float(N_KEYS) + pick(e2_ref, i2, True)
        ((best, eid),) = _top16_rows([cand], [cidx])
        ex = jnp.exp(best - best[0:1, :])
        gate = ex / jnp.sum(ex, axis=0, keepdims=True)
        dst = pl.ds(pl.multiple_of(hh * PEER_TOPK, PEER_TOPK), PEER_TOPK)
        ei_ref[lt, dst, :] = eid.astype(jnp.int32)
        g_ref[lt, dst, :] = gate

    def head(hh, carry):
        for lt in range(n_lt):
            job(hh, lt)
        return carry

    lax.fori_loop(0, PEER_HEADS, head, 0)
    for lt in range(n_lt):
        eidx_ref[0, lt * LANES:(lt + 1) * LANES, :] = ei_ref[lt].T
        gate_ref[0, lt * LANES:(lt + 1) * LANES, :] = g_ref[lt].T


def _route(x, s2, sh2, wqt, keys):
    b, seq, d = x.shape
    tr = ROUTE_TILE
    n_lt = tr // LANES
    e1, e2, n_cand = _candidate_tables()
    vec = pl.BlockSpec((1, 1, d), lambda bi, i: (bi, 0, 0))
    sel = pl.BlockSpec((CAND_ROWS, LANES), lambda bi, i: (0, 0))
    out = pl.BlockSpec((1, tr, PEER_SEL), lambda bi, i: (bi, i, 0))
    return pl.pallas_call(
        functools.partial(_route_kernel, tr=tr, n_cand=n_cand),
        grid=(b, seq // tr),
        in_specs=[
            pl.BlockSpec((1, tr, d), lambda bi, i: (bi, i, 0)),
            vec, vec,
            pl.BlockSpec(wqt.shape, lambda bi, i: (0, 0)),
            pl.BlockSpec(keys.shape, lambda bi, i: (0, 0, 0, 0)),
            sel, sel,
        ],
        out_specs=[out, out],
        out_shape=[jax.ShapeDtypeStruct((b, seq, PEER_SEL), jnp.int32),
                   jax.ShapeDtypeStruct((b, seq, PEER_SEL), F32)],
        scratch_shapes=[pltpu.VMEM((n_lt, 2 * PEER_HEADS, N_KEYS, LANES), BF16),
                        pltpu.VMEM((n_lt, PEER_SEL, LANES), jnp.int32),
                        pltpu.VMEM((n_lt, PEER_SEL, LANES), F32)],
        compiler_params=_params(("arbitrary", "arbitrary")),
        name="peer_route",
    )(x, s2, sh2, wqt, keys, jnp.asarray(e1), jnp.asarray(e2))


def _row_sums8(ps, sub):
    def merge(a, b, d):
        va = a + pltpu.roll(a, SUBLANES - d, axis=0)
        vb = b + pltpu.roll(b, d, axis=0)
        return jnp.where((sub & d) == 0, va, vb)
    y = [merge(ps[i], ps[i + 4], 4) for i in range(4)]
    x0 = merge(y[0], y[2], 2)
    x1 = merge(y[1], y[3], 2)
    return merge(x0, x1, 1)


def _expert_kernel(idx_ref, gate_ref, x_ref, s_ref, sh_ref, uv_ref, o_ref, *scratch, tb, nbuf):
    bufs = scratch[:nbuf]
    sem = scratch[nbuf]
    wb_ref = scratch[nbuf + 1]
    q_ref = scratch[nbuf + 2]
    w_ref = scratch[nbuf + 3]
    ahead = PEER_AHEAD
    assert 3 < ahead < nbuf
    s2 = 1.0 + s_ref[0]
    sh = sh_ref[0]
    sub = lax.broadcasted_iota(jnp.int32, (SUBLANES, LANES), 0)

    n_chunks = PEER_SEL // SUBLANES

    def issue_chunk(t, slot, c):
        for e in range(c * SUBLANES, (c + 1) * SUBLANES):
            pltpu.make_async_copy(uv_ref.at[idx_ref[t, e]], bufs[slot].at[e],
                                  sem.at[slot]).start(priority=e % 2)

    def wait(slot):
        pltpu.make_async_copy(uv_ref.at[pl.ds(0, PEER_SEL)], bufs[slot], sem.at[slot]).wait()

    def dots_chunk(xt, slot, c):
        ps = []
        for k in range(SUBLANES):
            p = bufs[slot][c * SUBLANES + k, 0:ROW_TILES, :].astype(F32) * xt
            ps.append(p[0:SUBLANES] + p[SUBLANES:2 * SUBLANES])
        return _row_sums8(ps, sub)

    def gated(t, par):
        a = jnp.sum(q_ref[par].T, axis=0, keepdims=True)
        act = 0.5 * a * (1.0 + lax.erf(a * np.float32(np.sqrt(0.5))))
        w_ref[par] = jnp.broadcast_to(act * gate_ref[pl.ds(t, 1), :], (SUBLANES, PEER_SEL))

    def spread(par):
        w = w_ref[par, 0:1, :]
        wb_ref[par] = jnp.broadcast_to(w, (LANES, PEER_SEL)).T

    def combine_chunk(accs, slot, c):
        for e in range(c * SUBLANES, (c + 1) * SUBLANES):
            term = (wb_ref[slot % 2, e:e + 1, :]
                    * bufs[slot][e, ROW_TILES:2 * ROW_TILES, :].astype(F32))
            accs[e % 4] = term if accs[e % 4] is None else accs[e % 4] + term

    def step(t, k, t_static=None):
        live = lambda d: t_static is None or 0 <= t_static + d < tb
        slot_i, slot_d = (k + ahead) % nbuf, (k + 3) % nbuf
        if live(3):
            wait(slot_d)
            xt = x_ref[t + 3] * s2 + sh
        if live(2):
            gated(t + 2, k % 2)
        if live(1):
            spread((k + 1) % 2)
        accs = [None] * 4
        for c in range(n_chunks):
            if live(ahead):
                issue_chunk(t + ahead, slot_i, c)
            if live(3):
                q_ref[(k + 1) % 2, c * SUBLANES:(c + 1) * SUBLANES, :] = dots_chunk(xt, slot_d, c)
            if live(0):
                combine_chunk(accs, k, c)
        if live(0):
            o_ref[t] = (accs[0] + accs[1]) + (accs[2] + accs[3])

    for t in range(ahead - 3):
        for c in range(n_chunks):
            issue_chunk(t, t, c)
    for t in range(-3, 0):
        step(t, t % nbuf, t)

    def group(gi, carry):
        for k in range(nbuf):
            step(gi * nbuf + k, k)
        return carry

    n_groups = tb // nbuf
    lax.fori_loop(0, n_groups - 1, group, 0)
    for k in range(nbuf):
        t = (n_groups - 1) * nbuf + k
        step(t, k, t)


def _experts(eidx, gate, x3, s2, sh2, uv, seq):
    t_all = x3.shape[0]
    tb, nbuf = PEER_TB, PEER_NBUF
    per_seq = seq // tb
    tok = pl.BlockSpec((tb, ROW_TILES, LANES), lambda i: (i, 0, 0))
    vec = pl.BlockSpec((1, ROW_TILES, LANES), lambda i: (i // per_seq, 0, 0))
    return pl.pallas_call(
        functools.partial(_expert_kernel, tb=tb, nbuf=nbuf),
        grid=(t_all // tb,),
        in_specs=[
            pl.BlockSpec((tb, PEER_SEL), lambda i: (i, 0), memory_space=pltpu.SMEM),
            pl.BlockSpec((tb, PEER_SEL), lambda i: (i, 0)),
            tok, vec, vec,
            pl.BlockSpec(memory_space=pl.ANY),
        ],
        out_specs=tok,
        out_shape=jax.ShapeDtypeStruct(x3.shape, F32),
        scratch_shapes=(
            [pltpu.VMEM((PEER_SEL, 2 * ROW_TILES, LANES), uv.dtype) for _ in range(nbuf)]
            + [pltpu.SemaphoreType.DMA((nbuf,)),
               pltpu.VMEM((2, PEER_SEL, LANES), F32),
               pltpu.VMEM((2, PEER_SEL, LANES), F32),
               pltpu.VMEM((2, SUBLANES, PEER_SEL), F32)]),
        compiler_params=_params(("arbitrary",)),
        name="peer_experts",
    )(eidx, gate, x3, s2, sh2, uv)


def _res_ln_kernel(x_ref, f_ref, g_ref, lg_ref, lb_ref, o_ref, *, alpha):
    y = alpha * x_ref[0] + g_ref[0] * f_ref[0]
    o_ref[0] = _layer_norm(y, lg_ref[...], lb_ref[...])


def _res_ln(x, f, g, lg, lb, alpha):
    b, seq, d = x.shape
    ts = TOKEN_TILE
    tok = pl.BlockSpec((1, ts, d), lambda bi, i: (bi, i, 0))
    row = pl.BlockSpec((1, d), lambda bi, i: (0, 0))
    return pl.pallas_call(
        functools.partial(_res_ln_kernel, alpha=alpha),
        grid=(b, seq // ts),
        in_specs=[tok, tok, pl.BlockSpec((1, 1, d), lambda bi, i: (bi, 0, 0)), row, row],
        out_specs=tok,
        out_shape=jax.ShapeDtypeStruct(x.shape, F32),
        compiler_params=_params(("arbitrary", "arbitrary")),
        name="res_ln",
    )(x, f, g, lg, lb)


def _trunk(x, mod, w):
    b, seq, d = x.shape
    depth = mod.shape[0]
    alpha = float((2 * depth) ** 0.25)
    for i in range(depth):
        sh1, s1, g1, sh2, s2, g2 = [m.reshape(b, 1, d) for m in jnp.split(mod[i], 6, axis=-1)]
        lw = w[i]
        if i % 2 == 0:
            x = _pool_layer(x, s1, sh1, g1, lw["pool_w"], lw["pool_scale"], lw["wo"],
                            lw["ln1_g"], lw["ln1_b"], alpha)
        else:
            q, k, v = _qkv(x, s1, sh1, lw["wqkv"])
            x = _attn_layer(x, q, k, v, g1, lw["sink"], lw["wo"], lw["ln1_g"], lw["ln1_b"], alpha)
        eidx, gate = _route(x, s2, sh2, lw["peer_wqt"], lw["peer_keys"])
        f3 = _experts(eidx.reshape(b * seq, PEER_SEL), gate.reshape(b * seq, PEER_SEL),
                      x.reshape(b * seq, ROW_TILES, LANES),
                      s2.reshape(b, ROW_TILES, LANES), sh2.reshape(b, ROW_TILES, LANES),
                      lw["uv"], seq)
        x = _res_ln(x, f3.reshape(b, seq, d), g2, lw["ln2_g"], lw["ln2_b"], alpha)
    return x


def kernel(x_prompt, x_sample, c_prompt, c_sample, ada_w, ada_b, ln1_g, ln1_b, ln2_g, ln2_b,
           pool_w, pool_scale, pool_wo, attn_wqkv, attn_sink, attn_wo,
           peer_wq, peer_keys, peer_u, peer_v):
    depth, d = ln1_g.shape
    n_exp = peer_u.shape[1]
    bp, bs = c_prompt.shape[0], c_sample.shape[0]
    m_pad = -(bp + bs) % SUBLANES
    c_all = jnp.concatenate([c_prompt, c_sample, jnp.zeros((m_pad, d), F32)], axis=0)
    mod = _ada(c_all, ada_w, ada_b)

    layers = []
    for i in range(depth):
        j = i // 2
        lw = {
            "ln1_g": ln1_g[i].reshape(1, d), "ln1_b": ln1_b[i].reshape(1, d),
            "ln2_g": ln2_g[i].reshape(1, d), "ln2_b": ln2_b[i].reshape(1, d),
            "peer_wqt": peer_wq[i].T.astype(BF16),
            "peer_keys": peer_keys[i].astype(BF16),
            "uv": jnp.concatenate([peer_u[i].astype(BF16), peer_v[i].astype(BF16)],
                                  axis=-1).reshape(n_exp, 2 * ROW_TILES, LANES),
        }
        if i % 2 == 0:
            lw.update(pool_w=pool_w[j].astype(BF16), pool_scale=pool_scale[j].reshape(1, d),
                      wo=pool_wo[j].astype(BF16))
        else:
            lw.update(wqkv=attn_wqkv[j].astype(BF16), sink=attn_sink[j],
                      wo=attn_wo[j].astype(BF16))
        layers.append(lw)

    y_prompt = _trunk(x_prompt, mod[:, :bp], layers)
    y_sample = _trunk(x_sample, mod[:, bp:bp + bs], layers)
    return (y_prompt, y_sample)
```
